```python
import jax, jax.numpy as jnp
from jax import lax
import numpy as np

D_MODEL = 1024
BATCH = 4
SEQ = 8192
DEPTH = 2

GRID_W = 64
CTX_LEN = 256
N_MIXERS = 2
EPS = 1e-6

MLA_HEADS = 8
MLA_Q_LORA = 384
MLA_KV_LORA = 256
MLA_NOPE = 128
MLA_ROPE = 64
MLA_V = 128
MLA_QK = MLA_NOPE + MLA_ROPE
ROPE_FREQ = MLA_ROPE // 4
ROPE_BASE = 10000.0
Q_BLOCK = 128

RW_HEAD = 64
RW_HEADS = D_MODEL // RW_HEAD
RW_DECAY_LORA = 64
RW_AAA_LORA = 64
RW_GATE_LORA = 160
RW_GN_EPS = 64e-5

D_FF = ((8 * D_MODEL // 3 + 255) // 256) * 256

kernel_name = 'hybrid_mla_rwkv7_prefix_dit'


def rms_norm(x, g, eps=EPS):
    xf = x.astype(jnp.float32)
    y = xf * lax.rsqrt(jnp.mean(xf * xf, axis=-1, keepdims=True) + eps)
    return (y * g.astype(jnp.float32)).astype(x.dtype)


def modulate(h, shift, scale):
    return h * (1 + scale) + shift


def swiglu(h, w1, w3, w2):
    return (jax.nn.silu(h @ w1) * (h @ w3)) @ w2


def axial_rope_tables(n_tokens):
    rows = n_tokens // GRID_W
    row = jnp.broadcast_to(jnp.arange(rows, dtype=jnp.float32)[:, None], (rows, GRID_W)).reshape(-1)
    col = jnp.broadcast_to(jnp.arange(GRID_W, dtype=jnp.float32)[None, :], (rows, GRID_W)).reshape(-1)
    inv = ROPE_BASE ** (-jnp.arange(ROPE_FREQ, dtype=jnp.float32) / ROPE_FREQ)
    ang = jnp.stack([row[:, None] * inv, col[:, None] * inv], axis=1)
    return jnp.cos(ang), jnp.sin(ang)


def apply_axial_rope(x, cos, sin):
    shp = x.shape
    xf = x.astype(jnp.float32).reshape(shp[:-1] + (2, 2, ROPE_FREQ))
    x1, x2 = xf[..., 0, :], xf[..., 1, :]
    c = cos[None, :, None]
    s = sin[None, :, None]
    out = jnp.stack([x1 * c - x2 * s, x1 * s + x2 * c], axis=-2)
    return out.reshape(shp).astype(x.dtype)


def mla_project(h, w_dqkv, g_q_lora, g_kv_lora, w_uq, w_ukv, g_qn, g_kn):
    B, L, _ = h.shape
    down = h @ w_dqkv
    cq = rms_norm(down[..., :MLA_Q_LORA], g_q_lora)
    ckv = rms_norm(down[..., MLA_Q_LORA:MLA_Q_LORA + MLA_KV_LORA], g_kv_lora)
    k_rope = down[..., MLA_Q_LORA + MLA_KV_LORA:]
    q = (cq @ w_uq).reshape(B, L, MLA_HEADS, MLA_QK)
    kv = (ckv @ w_ukv).reshape(B, L, MLA_HEADS, MLA_NOPE + MLA_V)
    k_nope, v = kv[..., :MLA_NOPE], kv[..., MLA_NOPE:]
    k = jnp.concatenate([k_nope, jnp.broadcast_to(k_rope[:, :, None, :], (B, L, MLA_HEADS, MLA_ROPE))], axis=-1)
    return rms_norm(q, g_qn), rms_norm(k, g_kn), v


def rope_tail(t, cos, sin):
    return jnp.concatenate([t[..., :MLA_NOPE], apply_axial_rope(t[..., MLA_NOPE:], cos, sin)], axis=-1)


def block_attention(q, k, v):
    B, Sq, H, Dq = q.shape
    nb = Sq // Q_BLOCK
    qb = q.reshape(B, nb, Q_BLOCK, H, Dq).transpose(1, 0, 2, 3, 4)
    scale = Dq ** -0.5

    def one(qblk):
        s = jnp.einsum('bqhd,bkhd->bhqk', qblk, k, preferred_element_type=jnp.float32) * scale
        p = jax.nn.softmax(s, axis=-1).astype(v.dtype)
        return jnp.einsum('bhqk,bkhd->bqhd', p, v)

    o = lax.map(one, qb)
    return o.transpose(1, 0, 2, 3, 4).reshape(B, Sq, H * v.shape[-1])


def mla_mixer(h_lat, h_ctx, w_dqkv, g_q_lora, g_kv_lora, w_uq, w_ukv, g_qn, g_kn, w_o, cos, sin, need_ctx_out):
    q_l, k_l, v_l = mla_project(h_lat, w_dqkv, g_q_lora, g_kv_lora, w_uq, w_ukv, g_qn, g_kn)
    q_l = rope_tail(q_l, cos, sin)
    k_l = rope_tail(k_l, cos, sin)
    q_c, k_c, v_c = mla_project(h_ctx, w_dqkv, g_q_lora, g_kv_lora, w_uq, w_ukv, g_qn, g_kn)
    k_all = jnp.concatenate([k_c, k_l], axis=1)
    v_all = jnp.concatenate([v_c, v_l], axis=1)
    o_l = block_attention(q_l, k_all, v_all) @ w_o
    o_c = block_attention(q_c, k_c, v_c) @ w_o if need_ctx_out else None
    return o_l, o_c


def centred_shift_delta(x):
    prev = jnp.pad(x[:, :-1], ((0, 0), (1, 0), (0, 0)))
    nxt = jnp.pad(x[:, 1:], ((0, 0), (0, 1), (0, 0)))
    return 0.5 * (prev + nxt) - x


def rwkv_features(h, need_out, mix, w_r, w_k, w_v, k_k, k_a, dw0, dw1, dw2, ia0, ia1, ia2, g1, g2):
    B, L, D = h.shape
    hd = (B, L, RW_HEADS, RW_HEAD)
    xx = centred_shift_delta(h)
    xr, xw, xk, xv, xa, xg = [h + xx * mix[j] for j in range(6)]
    k = xk @ w_k
    v = (xv @ w_v).reshape(hd)
    kk = (k * k_k).reshape(hd).astype(jnp.float32)
    kk = kk * lax.rsqrt(jnp.sum(kk * kk, axis=-1, keepdims=True) + 1e-12)
    dirs = []
    for d in range(2):
        w_log = -jax.nn.softplus(-(dw0[d] + jnp.tanh(xw @ dw1[d]) @ dw2[d]).astype(jnp.float32)) - 0.5
        decay = jnp.exp(-jnp.exp(w_log)).reshape(hd)
        a = jax.nn.sigmoid((ia0[d] + (xa @ ia1[d]) @ ia2[d]).astype(jnp.float32))
        k_d = (k * (1 + (a - 1) * k_a)).reshape(hd)
        a = a.reshape(hd)
        dirs.append((decay, k_d, -kk, kk * a))
    if need_out:
        r = (xr @ w_r).reshape(hd)
        g = jax.nn.sigmoid(xg @ g1) @ g2
    else:
        r, g = None, None
    return r, v, g, dirs


def wkv7_scan(r, w, k, v, a, b, state0, reverse):
    emit = r is not None
    xs = [w, k, v, a, b] + ([r] if emit else [])
    xs = tuple(jnp.moveaxis(t.astype(jnp.float32), 1, 0) for t in xs)

    def step(S, inp):
        w_t, k_t, v_t, a_t, b_t = inp[:5]
        Sa = jnp.einsum('bhvk,bhk->bhv', S, a_t)
        S = S * w_t[:, :, None, :] + Sa[..., None] * b_t[:, :, None, :] + v_t[..., None] * k_t[:, :, None, :]
        y = jnp.einsum('bhvk,bhk->bhv', S, inp[5]) if emit else None
        return S, y

    S, ys = lax.scan(step, state0, xs, reverse=reverse)
    return S, (jnp.moveaxis(ys, 0, 1) if emit else None)


def rwkv_output(y, r, v, g, dirs, r_k, gn_w, gn_b, w_o):
    B, L, H, N = y.shape
    mu = jnp.mean(y, axis=-1, keepdims=True)
    var = jnp.mean(jnp.square(y - mu), axis=-1, keepdims=True)
    yn = (y - mu) * lax.rsqrt(var + RW_GN_EPS) * gn_w.reshape(H, N) + gn_b.reshape(H, N)
    bonus = sum(jnp.sum(r * dd[1] * r_k, axis=-1, keepdims=True) for dd in dirs) * v
    return ((yn + bonus).reshape(B, L, H * N).astype(g.dtype) * g) @ w_o


def rwkv_mixer(h_lat, h_ctx, mix, w_r, w_k, w_v, w_o, k_k, k_a, r_k, dw0, dw1, dw2, ia0, ia1, ia2, g1, g2, gn_w, gn_b, need_ctx_out):
    fp = (mix, w_r, w_k, w_v, k_k, k_a, dw0, dw1, dw2, ia0, ia1, ia2, g1, g2)
    r_l, v_l, g_l, dirs_l = rwkv_features(h_lat, True, *fp)
    r_c, v_c, g_c, dirs_c = rwkv_features(h_ctx, need_ctx_out, *fp)
    B = h_lat.shape[0]
    y_l, y_c = 0.0, 0.0
    for d, rev in enumerate((False, True)):
        state0 = jnp.zeros((B, RW_HEADS, RW_HEAD, RW_HEAD), jnp.float32)
        dc, kc, ac, bc = dirs_c[d]
        S_c, yc = wkv7_scan(r_c, dc, kc, v_c, ac, bc, state0, rev)
        dl, kl, al, bl = dirs_l[d]
        _, yl = wkv7_scan(r_l, dl, kl, v_l, al, bl, S_c, rev)
        y_l = y_l + yl
        if need_ctx_out:
            y_c = y_c + yc
    o_l = rwkv_output(y_l, r_l, v_l, g_l, dirs_l, r_k, gn_w, gn_b, w_o)
    o_c = rwkv_output(y_c, r_c, v_c, g_c, dirs_c, r_k, gn_w, gn_b, w_o) if need_ctx_out else None
    return o_l, o_c


def setup_inputs(seed: int = 0) -> dict:
    key = jax.random.key(seed)
    ks = iter(jax.random.split(key, 48))
    f32 = jnp.float32

    def nrm(shape, scale):
        return jax.random.normal(next(ks), shape, f32) * scale

    def gain(shape):
        return 1.0 + 0.05 * jax.random.normal(next(ks), shape, f32)

    def unif(shape, lo, hi):
        return jax.random.uniform(next(ks), shape, f32, lo, hi)

    D = D_MODEL
    n_a = (DEPTH + N_MIXERS - 1) // N_MIXERS
    n_b = DEPTH // N_MIXERS
    return {
        'x': nrm((BATCH, SEQ, D), 1.0),
        'c': nrm((BATCH, D), 1.0),
        'ctx': nrm((BATCH, CTX_LEN, D), 1.0),
        'c_ctx': nrm((D,), 1.0),
        'ada_w': nrm((DEPTH, D, 6 * D), 0.5 * D ** -0.5),
        'ada_b': nrm((DEPTH, 6 * D), 0.02),
        'norm_mix': gain((DEPTH, D)),
        'norm_ffn': gain((DEPTH, D)),
        'ffn_w1': nrm((DEPTH, D, D_FF), D ** -0.5),
        'ffn_w3': nrm((DEPTH, D, D_FF), D ** -0.5),
        'ffn_w2': nrm((DEPTH, D_FF, D), D_FF ** -0.5),
        'mla_w_dqkv': nrm((n_a, D, MLA_Q_LORA + MLA_KV_LORA + MLA_ROPE), D ** -0.5),
        'mla_g_q_lora': gain((n_a, MLA_Q_LORA)),
        'mla_g_kv_lora': gain((n_a, MLA_KV_LORA)),
        'mla_w_uq': nrm((n_a, MLA_Q_LORA, MLA_HEADS * MLA_QK), MLA_Q_LORA ** -0.5),
        'mla_w_ukv': nrm((n_a, MLA_KV_LORA, MLA_HEADS * (MLA_NOPE + MLA_V)), MLA_KV_LORA ** -0.5),
        'mla_g_qn': gain((n_a, MLA_QK)),
        'mla_g_kn': gain((n_a, MLA_QK)),
        'mla_w_o': nrm((n_a, MLA_HEADS * MLA_V, D), (MLA_HEADS * MLA_V) ** -0.5),
        'rw_mix': unif((n_b, 6, D), 0.0, 1.0),
        'rw_w_r': nrm((n_b, D, D), D ** -0.5),
        'rw_w_k': nrm((n_b, D, D), D ** -0.5),
        'rw_w_v': nrm((n_b, D, D), D ** -0.5),
        'rw_w_o': nrm((n_b, D, D), D ** -0.5),
        'rw_k_k': 0.85 + nrm((n_b, D), 0.05),
        'rw_k_a': gain((n_b, D)),
        'rw_r_k': nrm((n_b, RW_HEADS, RW_HEAD), 0.1),
        'rw_decay_w0': unif((n_b, 2, D), -6.0, 0.0),
        'rw_decay_w1': nrm((n_b, 2, D, RW_DECAY_LORA), D ** -0.5),
        'rw_decay_w2': nrm((n_b, 2, RW_DECAY_LORA, D), 0.5 * RW_DECAY_LORA ** -0.5),
        'rw_iclr_a0': nrm((n_b, 2, D), 0.5),
        'rw_iclr_a1': nrm((n_b, 2, D, RW_AAA_LORA), D ** -0.5),
        'rw_iclr_a2': nrm((n_b, 2, RW_AAA_LORA, D), 0.5 * RW_AAA_LORA ** -0.5),
        'rw_gate_g1': nrm((n_b, D, RW_GATE_LORA), D ** -0.5),
        'rw_gate_g2': nrm((n_b, RW_GATE_LORA, D), RW_GATE_LORA ** -0.5),
        'rw_gn_w': gain((n_b, D)),
        'rw_gn_b': nrm((n_b, D), 0.02),
    }


def reference(x, c, ctx, c_ctx, ada_w, ada_b, norm_mix, norm_ffn, ffn_w1, ffn_w3, ffn_w2,
              mla_w_dqkv, mla_g_q_lora, mla_g_kv_lora, mla_w_uq, mla_w_ukv, mla_g_qn, mla_g_kn, mla_w_o,
              rw_mix, rw_w_r, rw_w_k, rw_w_v, rw_w_o, rw_k_k, rw_k_a, rw_r_k,
              rw_decay_w0, rw_decay_w1, rw_decay_w2, rw_iclr_a0, rw_iclr_a1, rw_iclr_a2,
              rw_gate_g1, rw_gate_g2, rw_gn_w, rw_gn_b):
    n_lat = x.shape[1]
    cos, sin = axial_rope_tables(n_lat)
    for i in range(DEPTH):
        last = i == DEPTH - 1
        j = i // N_MIXERS
        mod_l = (jax.nn.silu(c) @ ada_w[i] + ada_b[i])[:, None, :]
        mod_c = jax.nn.silu(c_ctx) @ ada_w[i] + ada_b[i]
        sh_m, sc_m, ga_m, sh_f, sc_f, ga_f = jnp.split(mod_l, 6, axis=-1)
        csh_m, csc_m, cga_m, csh_f, csc_f, cga_f = jnp.split(mod_c, 6, axis=-1)
        h_l = modulate(rms_norm(x, norm_mix[i]), sh_m, sc_m)
        h_c = modulate(rms_norm(ctx, norm_mix[i]), csh_m, csc_m)
        if i % N_MIXERS == 0:
            o_l, o_c = mla_mixer(h_l, h_c, mla_w_dqkv[j], mla_g_q_lora[j], mla_g_kv_lora[j], mla_w_uq[j],
                                 mla_w_ukv[j], mla_g_qn[j], mla_g_kn[j], mla_w_o[j], cos, sin, not last)
        else:
            o_l, o_c = rwkv_mixer(h_l, h_c, rw_mix[j], rw_w_r[j], rw_w_k[j], rw_w_v[j], rw_w_o[j], rw_k_k[j],
                                  rw_k_a[j], rw_r_k[j], rw_decay_w0[j], rw_decay_w1[j], rw_decay_w2[j],
                                  rw_iclr_a0[j], rw_iclr_a1[j], rw_iclr_a2[j], rw_gate_g1[j], rw_gate_g2[j],
                                  rw_gn_w[j], rw_gn_b[j], not last)
        x = x + ga_m * o_l
        x = x + ga_f * swiglu(modulate(rms_norm(x, norm_ffn[i]), sh_f, sc_f), ffn_w1[i], ffn_w3[i], ffn_w2[i])
        if not last:
            ctx = ctx + cga_m * o_c
            ctx = ctx + cga_f * swiglu(modulate(rms_norm(ctx, norm_ffn[i]), csh_f, csc_f), ffn_w1[i], ffn_w3[i], ffn_w2[i])
    return x
```

```python
import functools

import jax
import jax.numpy as jnp
import numpy as np
from jax import lax
from jax.experimental import pallas as pl
from jax.experimental.pallas import tpu as pltpu

F32 = jnp.float32
BF16 = jnp.bfloat16
HIGHEST = lax.Precision.HIGHEST

EPS = 1e-6
GRID_W = 64
MLA_HEADS = 8
MLA_Q_LORA = 384
MLA_KV_LORA = 256
MLA_NOPE = 128
MLA_ROPE = 64
MLA_V = 128
MLA_QK = MLA_NOPE + MLA_ROPE
MLA_QK_PAD = 256
ROPE_FREQ = MLA_ROPE // 4
ROPE_BASE = 10000.0
RW_HEAD = 64
RW_GATE_LORA = 160
RW_GATE_PAD = 256
RW_GN_EPS = 64e-5
WKV_CHUNK = 64

VMEM_LIMIT_BYTES = 56 * 1024 * 1024


def _params(sem):
    return pltpu.CompilerParams(dimension_semantics=sem, vmem_limit_bytes=VMEM_LIMIT_BYTES)


def _dot(a, b):
    return jnp.dot(a, b, preferred_element_type=F32)


def _dot_nt(a, b):
    return lax.dot_general(a, b, (((1,), (1,)), ((), ())), preferred_element_type=F32)


def _dot_tn(a, b):
    return lax.dot_general(a, b, (((0,), (0,)), ((), ())), preferred_element_type=F32)


def _norm_mod(x, g, shift, scale):
    ms = jnp.mean(x * x, axis=-1, keepdims=True)
    return (x * lax.rsqrt(ms + EPS) * g) * (1.0 + scale) + shift


def _rms(x, g):
    ms = jnp.mean(x * x, axis=-1, keepdims=True)
    return x * lax.rsqrt(ms + EPS) * g


def _full(shape):
    n = len(shape)
    return pl.BlockSpec(shape, lambda *_: (0,) * n)


def _mod_kernel(cc_ref, w_ref, b_ref, o_ref):
    cc = cc_ref[...]
    s = cc * jax.nn.sigmoid(cc)
    o_ref[0] = jnp.dot(s, w_ref[0], precision=HIGHEST, preferred_element_type=F32) + b_ref[0]


def _adaln_mod(cc, ada_w, ada_b):
    depth, d, n = ada_w.shape
    tn = 1536
    rows = cc.shape[0]
    return pl.pallas_call(
        _mod_kernel,
        grid=(depth, n // tn),
        in_specs=[
            pl.BlockSpec((rows, d), lambda i, j: (0, 0)),
            pl.BlockSpec((1, d, tn), lambda i, j: (i, 0, j)),
            pl.BlockSpec((1, 1, tn), lambda i, j: (i, 0, j)),
        ],
        out_specs=pl.BlockSpec((1, rows, tn), lambda i, j: (i, 0, j)),
        out_shape=jax.ShapeDtypeStruct((depth, rows, n), F32),
        compiler_params=_params(("parallel", "parallel")),
        name="adaln_mod",
    )(cc, ada_w, ada_b.reshape(depth, 1, n))


def _mla_proj_kernel(x_ref, mod_ref, gmix_ref, wd_ref, glq_ref, glkv_ref, wuq_ref, wukv_ref,
                     gq_ref, gk_ref, tab_ref, q_ref, k_ref, v_ref):
    mod = mod_ref[0]
    h = _norm_mod(x_ref[0], gmix_ref[...], mod[0:1], mod[1:2])
    down = _dot(h.astype(BF16), wd_ref[...])
    cq = _rms(down[:, :MLA_Q_LORA], glq_ref[...]).astype(BF16)
    ckv = _rms(down[:, MLA_Q_LORA:MLA_Q_LORA + MLA_KV_LORA], glkv_ref[...]).astype(BF16)
    kr = down[:, MLA_Q_LORA + MLA_KV_LORA:]
    tab = tab_ref[...]
    lo = lax.broadcasted_iota(jnp.int32, (1, 128), 1) < MLA_ROPE
    gq = gq_ref[...]
    gk = gk_ref[...]

    def rope(t):
        p = t * tab
        return jnp.where(lo, p + pltpu.roll(p, MLA_ROPE, 1), 0.0)

    kr_ss = jnp.sum(jnp.where(lo, kr * kr, 0.0), axis=-1, keepdims=True)
    k_rot = rope(kr * gk[:, 128:])
    for hd in range(MLA_HEADS):
        c0 = hd * MLA_QK_PAD
        qh = _dot(cq, wuq_ref[:, c0:c0 + MLA_QK_PAD])
        qn, qr = qh[:, :128], qh[:, 128:]
        ssq = jnp.sum(qn * qn, axis=-1, keepdims=True) + jnp.sum(jnp.where(lo, qr * qr, 0.0), axis=-1, keepdims=True)
        rq = lax.rsqrt(ssq * (1.0 / MLA_QK) + EPS) * (MLA_QK ** -0.5)
        q_ref[0, hd, :, :128] = (qn * gq[:, :128] * rq).astype(BF16)
        q_ref[0, hd, :, 128:] = (rope(qr * gq[:, 128:]) * rq).astype(BF16)
        kvh = _dot(ckv, wukv_ref[:, c0:c0 + MLA_QK_PAD])
        kn = kvh[:, :128]
        rk = lax.rsqrt((jnp.sum(kn * kn, axis=-1, keepdims=True) + kr_ss) * (1.0 / MLA_QK) + EPS)
        k_ref[0, hd, :, :128] = (kn * gk[:, :128] * rk).astype(BF16)
        k_ref[0, hd, :, 128:] = (k_rot * rk).astype(BF16)
        v_ref[0, hd] = kvh[:, 128:].astype(BF16)


def _mla_proj(x, mod, gmix, wts, tab, tm):
    b, l, d = x.shape
    hh = MLA_HEADS
    per_batch_mod = mod.shape[0] > 1
    mod_map = (lambda bi, i: (bi, 0, 0)) if per_batch_mod else (lambda bi, i: (0, 0, 0))
    wd, glq, glkv, wuq, wukv, gq, gk = wts
    return pl.pallas_call(
        _mla_proj_kernel,
        grid=(b, l // tm),
        in_specs=[
            pl.BlockSpec((1, tm, d), lambda bi, i: (bi, i, 0)),
            pl.BlockSpec((1, 6, d), mod_map),
            _full(gmix.shape), _full(wd.shape), _full(glq.shape), _full(glkv.shape),
            _full(wuq.shape), _full(wukv.shape), _full(gq.shape), _full(gk.shape),
            pl.BlockSpec((tm, 128), lambda bi, i: (i, 0)),
        ],
        out_specs=[
            pl.BlockSpec((1, hh, tm, MLA_QK_PAD), lambda bi, i: (bi, 0, i, 0)),
            pl.BlockSpec((1, hh, tm, MLA_QK_PAD), lambda bi, i: (bi, 0, i, 0)),
            pl.BlockSpec((1, hh, tm, MLA_V), lambda bi, i: (bi, 0, i, 0)),
        ],
        out_shape=[
            jax.ShapeDtypeStruct((b, hh, l, MLA_QK_PAD), BF16),
            jax.ShapeDtypeStruct((b, hh, l, MLA_QK_PAD), BF16),
            jax.ShapeDtypeStruct((b, hh, l, MLA_V), BF16),
        ],
        compiler_params=_params(("parallel", "parallel")),
        name="mla_proj",
    )(x, mod, gmix, wd, glq, glkv, wuq, wukv, gq, gk, tab)


def _flash_kernel(*refs, has_ctx):
    if has_ctx:
        q_ref, k_ref, v_ref, kc_ref, vc_ref, o_ref, m_sc, l_sc, acc_sc = refs
    else:
        q_ref, k_ref, v_ref, o_ref, m_sc, l_sc, acc_sc = refs
    j = pl.program_id(3)
    q = q_ref[0, 0]

    def attend(k, v):
        s = _dot_nt(q, k)
        m_prev = m_sc[...]
        m_new = jnp.maximum(m_prev, jnp.max(s, axis=-1, keepdims=True))
        alpha = jnp.exp(m_prev - m_new)
        p = jnp.exp(s - m_new)
        l_sc[...] = alpha * l_sc[...] + jnp.sum(p, axis=-1, keepdims=True)
        acc_sc[...] = alpha * acc_sc[...] + _dot(p.astype(BF16), v)
        m_sc[...] = m_new

    @pl.when(j == 0)
    def _():
        m_sc[...] = jnp.full(m_sc.shape, -1e30, F32)
        l_sc[...] = jnp.zeros(l_sc.shape, F32)
        acc_sc[...] = jnp.zeros(acc_sc.shape, F32)
        if has_ctx:
            attend(kc_ref[0, 0], vc_ref[0, 0])

    attend(k_ref[0, 0], v_ref[0, 0])

    @pl.when(j == pl.num_programs(3) - 1)
    def _():
        o_ref[0] = (acc_sc[...] / l_sc[...]).astype(o_ref.dtype)


def _flash(q, k, v, kc, vc, tq, tk):
    b, hh, sq, dq = q.shape
    lk = k.shape[2]
    dv = v.shape[3]
    has_ctx = kc is not None
    in_specs = [
        pl.BlockSpec((1, 1, tq, dq), lambda bi, h, i, j: (bi, h, i, 0)),
        pl.BlockSpec((1, 1, tk, dq), lambda bi, h, i, j: (bi, h, j, 0)),
        pl.BlockSpec((1, 1, tk, dv), lambda bi, h, i, j: (bi, h, j, 0)),
    ]
    args = [q, k, v]
    if has_ctx:
        lc = kc.shape[2]
        in_specs += [
            pl.BlockSpec((1, 1, lc, dq), lambda bi, h, i, j: (bi, h, 0, 0)),
            pl.BlockSpec((1, 1, lc, dv), lambda bi, h, i, j: (bi, h, 0, 0)),
        ]
        args += [kc, vc]
    return pl.pallas_call(
        functools.partial(_flash_kernel, has_ctx=has_ctx),
        grid=(b, hh, sq // tq, lk // tk),
        in_specs=in_specs,
        out_specs=pl.BlockSpec((1, tq, dv), lambda bi, h, i, j: (bi, i, h)),
        out_shape=jax.ShapeDtypeStruct((b, sq, hh * dv), BF16),
        scratch_shapes=[pltpu.VMEM((tq, 1), F32), pltpu.VMEM((tq, 1), F32), pltpu.VMEM((tq, dv), F32)],
        compiler_params=_params(("parallel", "parallel", "parallel", "arbitrary")),
        name="mla_flash_ctx" if has_ctx else "mla_flash",
    )(*args)


def _post_ffn_kernel(x_ref, z_ref, mod_ref, wo_ref, gffn_ref, w1_ref, w3_ref, w2_ref, o_ref,
                     x1_sc, h_sc, acc_sc):
    f = pl.program_id(2)
    mod = mod_ref[0]

    @pl.when(f == 0)
    def _():
        x1 = x_ref[0] + mod[2:3] * _dot(z_ref[0], wo_ref[...])
        x1_sc[...] = x1
        h_sc[...] = _norm_mod(x1, gffn_ref[...], mod[3:4], mod[4:5]).astype(BF16)
        acc_sc[...] = jnp.zeros(acc_sc.shape, F32)

    h = h_sc[...]
    a = _dot(h, w1_ref[...])
    g = (a * jax.nn.sigmoid(a)) * _dot(h, w3_ref[...])
    acc_sc[...] += _dot(g.astype(BF16), w2_ref[...])

    @pl.when(f == pl.num_programs(2) - 1)
    def _():
        o_ref[0] = x1_sc[...] + mod[5:6] * acc_sc[...]


def _post_ffn(x, z, mod, wo, gffn, w1, w3, w2, tm, tf):
    b, l, d = x.shape
    dff = w1.shape[1]
    per_batch_mod = mod.shape[0] > 1
    mod_map = (lambda bi, i, f: (bi, 0, 0)) if per_batch_mod else (lambda bi, i, f: (0, 0, 0))
    return pl.pallas_call(
        _post_ffn_kernel,
        grid=(b, l // tm, dff // tf),
        in_specs=[
            pl.BlockSpec((1, tm, d), lambda bi, i, f: (bi, i, 0)),
            pl.BlockSpec((1, tm, d), lambda bi, i, f: (bi, i, 0)),
            pl.BlockSpec((1, 6, d), mod_map),
            pl.BlockSpec((d, d), lambda bi, i, f: (0, 0)),
            pl.BlockSpec((1, d), lambda bi, i, f: (0, 0)),
            pl.BlockSpec((d, tf), lambda bi, i, f: (0, f)),
            pl.BlockSpec((d, tf), lambda bi, i, f: (0, f)),
            pl.BlockSpec((tf, d), lambda bi, i, f: (f, 0)),
        ],
        out_specs=pl.BlockSpec((1, tm, d), lambda bi, i, f: (bi, i, 0)),
        out_shape=jax.ShapeDtypeStruct((b, l, d), F32),
        scratch_shapes=[pltpu.VMEM((tm, d), F32), pltpu.VMEM((tm, d), BF16), pltpu.VMEM((tm, d), F32)],
        compiler_params=_params(("parallel", "parallel", "arbitrary")),
        name="post_ffn",
    )(x, z, mod, wo, gffn, w1, w3, w2)


def _rwkv_feat_kernel(x_ref, xp_ref, xn_ref, mod_ref, gmix_ref, mix_ref, vec_ref, wr_ref, wk_ref, wv_ref,
                      dw1_ref, dw2_ref, ia1_ref, ia2_ref, g1_ref, g2_ref, hsum_ref,
                      r_ref, v_ref, g_ref, kk_ref, kd_ref, lw_ref, b_ref):
    i = pl.program_id(1)
    nt = pl.num_programs(1)
    mod = mod_ref[0]
    gmix = gmix_ref[...]
    tm, d = x_ref.shape[1], x_ref.shape[2]
    h = _norm_mod(x_ref[0], gmix, mod[0:1], mod[1:2])
    hp = _norm_mod(xp_ref[0], gmix, mod[0:1], mod[1:2])[7:8]
    hn = _norm_mod(xn_ref[0], gmix, mod[0:1], mod[1:2])[0:1]
    hp = jnp.where(i > 0, hp, 0.0)
    hn = jnp.where(i < nt - 1, hn, 0.0)
    row = lax.broadcasted_iota(jnp.int32, (tm, 1), 0)
    prev = jnp.where(row == 0, hp, pltpu.roll(h, 1, 0))
    nxt = jnp.where(row == tm - 1, hn, pltpu.roll(h, tm - 1, 0))
    xx = 0.5 * (prev + nxt) - h
    mix = mix_ref[...]
    vec = vec_ref[...]

    def mixed(jj):
        return (h + xx * mix[jj:jj + 1]).astype(BF16)

    r_ref[0] = _dot(mixed(0), wr_ref[...])
    k = _dot(mixed(2), wk_ref[...])
    v_ref[0] = _dot(mixed(3), wv_ref[...])
    g_ref[0] = _dot(jax.nn.sigmoid(_dot(mixed(5), g1_ref[...])).astype(BF16), g2_ref[...])
    dd = _dot(jnp.tanh(_dot(mixed(1), dw1_ref[...])).astype(BF16), dw2_ref[...])
    aa = _dot(_dot(mixed(4), ia1_ref[...]).astype(BF16), ia2_ref[...])
    kkr = k * vec[0:1]
    kk = kkr * lax.rsqrt(_dot((kkr * kkr).astype(BF16), hsum_ref[...]) + 1e-12)
    kk_ref[0] = kk
    for dr in range(2):
        z = -(vec[2 + dr:3 + dr] + dd[:, dr * d:(dr + 1) * d])
        softplus = jnp.maximum(z, 0.0) + jnp.log(1.0 + jnp.exp(-jnp.abs(z)))
        lw_ref[dr, 0] = -jnp.exp(-softplus - 0.5)
        a = jax.nn.sigmoid(vec[4 + dr:5 + dr] + aa[:, dr * d:(dr + 1) * d])
        kd_ref[dr, 0] = k * (1.0 + (a - 1.0) * vec[1:2])
        b_ref[dr, 0] = kk * a


def _rwkv_feat(x, mod, gmix, mix, vec, wts, tm):
    b, l, d = x.shape
    per_batch_mod = mod.shape[0] > 1
    mod_map = (lambda bi, i: (bi, 0, 0)) if per_batch_mod else (lambda bi, i: (0, 0, 0))
    t8 = tm // 8
    last8 = l // 8 - 1
    tok = pl.BlockSpec((1, tm, d), lambda bi, i: (bi, i, 0))
    tok2 = pl.BlockSpec((2, 1, tm, d), lambda bi, i: (0, bi, i, 0))
    one = jax.ShapeDtypeStruct((b, l, d), F32)
    two = jax.ShapeDtypeStruct((2, b, l, d), F32)
    return pl.pallas_call(
        _rwkv_feat_kernel,
        grid=(b, l // tm),
        in_specs=[
            tok,
            pl.BlockSpec((1, 8, d), lambda bi, i: (bi, jnp.maximum(i * t8 - 1, 0), 0)),
            pl.BlockSpec((1, 8, d), lambda bi, i: (bi, jnp.minimum((i + 1) * t8, last8), 0)),
            pl.BlockSpec((1, 6, d), mod_map),
            _full(gmix.shape), _full(mix.shape), _full(vec.shape),
        ] + [_full(w.shape) for w in wts],
        out_specs=[tok, tok, tok, tok, tok2, tok2, tok2],
        out_shape=[one, one, one, one, two, two, two],
        compiler_params=_params(("parallel", "parallel")),
        name="rwkv_feat",
    )(x, x, x, mod, gmix, mix, vec, *wts)


def _wkv_kernel(r_ref, v_ref, kk_ref, kd_ref, lw_ref, b_ref, s0_ref, y_ref, sT_ref, st_sc):
    dr = pl.program_id(0)
    s = pl.program_id(2)
    c = WKV_CHUNK
    n = RW_HEAD
    d = r_ref.shape[2]
    nh = d // n

    @pl.when(s == 0)
    def _():
        st_sc[...] = s0_ref[0, 0]

    sign = 1 - 2 * dr
    tt = lax.broadcasted_iota(jnp.int32, (c, c), 0)
    ss = lax.broadcasted_iota(jnp.int32, (c, c), 1)
    ahead = (tt - ss) * sign
    incl = ahead >= 0
    strict = ahead > 0
    eye = (tt == ss).astype(F32)

    lw = lw_ref[0, 0]
    cum = jnp.dot(incl.astype(F32), lw, precision=HIGHEST, preferred_element_type=F32)
    e_pos = jnp.exp(cum)
    e_neg = jnp.exp(-cum)
    last = jnp.where(dr == 0, c - 1, 0)
    rowsel = lax.broadcasted_iota(jnp.int32, (c, 1), 0) == last
    w_end = jnp.sum(jnp.where(rowsel, e_pos, 0.0), axis=0, keepdims=True)
    rt = r_ref[0] * e_pos
    at = -kk_ref[0] * jnp.exp(cum - lw)
    bt = b_ref[0, 0] * e_neg
    kt = kd_ref[0, 0] * e_neg
    bw = bt * w_end
    kw = kt * w_end
    v = v_ref[0]

    for hd in range(nh):
        sl = slice(hd * n, (hd + 1) * n)
        s0 = st_sc[hd]
        ar = jnp.concatenate([at[:, sl], rt[:, sl]], axis=0).astype(BF16)
        bk = jnp.concatenate([bt[:, sl], kt[:, sl]], axis=0).astype(BF16)
        p = _dot_nt(ar, bk)
        l_ab = jnp.where(strict, p[:c, :c], 0.0)
        l_ak = jnp.where(strict, p[:c, c:], 0.0)
        m_rb = jnp.where(incl, p[c:, :c], 0.0)
        m_rk = jnp.where(incl, p[c:, c:], 0.0)
        tinv = eye + l_ab
        lp = l_ab
        for _ in range(5):
            lp = _dot(lp.astype(BF16), lp.astype(BF16))
            tinv = tinv + _dot(tinv.astype(BF16), lp.astype(BF16))
        vh = v[:, sl]
        vb = vh.astype(BF16)
        ars = _dot_nt(ar, s0.astype(BF16))
        x = ars[:c] + _dot(l_ak.astype(BF16), vb)
        u = _dot(tinv.astype(BF16), x.astype(BF16))
        uv = jnp.concatenate([u, vh], axis=0).astype(BF16)
        m = jnp.concatenate([m_rb, m_rk], axis=1).astype(BF16)
        y_ref[0, 0, :, sl] = ars[c:] + _dot(m, uv)
        bkw = jnp.concatenate([bw[:, sl], kw[:, sl]], axis=0).astype(BF16)
        st_sc[hd] = s0 * w_end[:, sl] + _dot_tn(uv, bkw)

    @pl.when(s == pl.num_programs(2) - 1)
    def _():
        sT_ref[0, 0] = st_sc[...]


def _wkv(r, v, kk, kd, lw, bb, s0):
    b, l, d = r.shape
    c = WKV_CHUNK
    nc = l // c
    nh = d // RW_HEAD

    def cidx(dr, s):
        return s + dr * (nc - 1 - 2 * s)

    tok = pl.BlockSpec((1, c, d), lambda dr, bi, s: (bi, cidx(dr, s), 0))
    tok2 = pl.BlockSpec((1, 1, c, d), lambda dr, bi, s: (dr, bi, cidx(dr, s), 0))
    st = pl.BlockSpec((1, 1, nh, RW_HEAD, RW_HEAD), lambda dr, bi, s: (dr, bi, 0, 0, 0))
    return pl.pallas_call(
        _wkv_kernel,
        grid=(2, b, nc),
        in_specs=[tok, tok, tok, tok2, tok2, tok2, st],
        out_specs=[tok2, st],
        out_shape=[jax.ShapeDtypeStruct((2, b, l, d), F32),
                   jax.ShapeDtypeStruct((2, b, nh, RW_HEAD, RW_HEAD), F32)],
        scratch_shapes=[pltpu.VMEM((nh, RW_HEAD, RW_HEAD), F32)],
        compiler_params=_params(("parallel", "parallel", "arbitrary")),
        name="wkv7",
    )(r, v, kk, kd, lw, bb, s0)


def _rwkv_out_kernel(y_ref, r_ref, kd_ref, v_ref, g_ref, vec_ref, hsum_ref, z_ref):
    vec = vec_ref[...]
    hsum = hsum_ref[...]
    inv_n = 1.0 / RW_HEAD
    y = y_ref[0, 0] + y_ref[1, 0]
    mu = _dot(y.astype(BF16), hsum) * inv_n
    yc = y - mu
    var = _dot((yc * yc).astype(BF16), hsum) * inv_n
    yn = yc * lax.rsqrt(var + RW_GN_EPS) * vec[1:2] + vec[2:3]
    rk = r_ref[0] * (kd_ref[0, 0] + kd_ref[1, 0]) * vec[0:1]
    bonus = _dot(rk.astype(BF16), hsum) * v_ref[0]
    z_ref[0] = ((yn + bonus) * g_ref[0]).astype(z_ref.dtype)


def _rwkv_out(y, r, kd, v, g, vec, hsum, tm):
    b, l, d = r.shape
    tok = pl.BlockSpec((1, tm, d), lambda bi, i: (bi, i, 0))
    tok2 = pl.BlockSpec((2, 1, tm, d), lambda bi, i: (0, bi, i, 0))
    return pl.pallas_call(
        _rwkv_out_kernel,
        grid=(b, l // tm),
        in_specs=[tok2, tok, tok2, tok, tok, _full(vec.shape), _full(hsum.shape)],
        out_specs=tok,
        out_shape=jax.ShapeDtypeStruct((b, l, d), BF16),
        compiler_params=_params(("parallel", "parallel")),
        name="rwkv_out",
    )(y, r, kd, v, g, vec, hsum)


def _rope_table(n_tokens):
    rows = n_tokens // GRID_W
    row = jnp.broadcast_to(jnp.arange(rows, dtype=F32)[:, None], (rows, GRID_W)).reshape(-1)
    col = jnp.broadcast_to(jnp.arange(GRID_W, dtype=F32)[None, :], (rows, GRID_W)).reshape(-1)
    inv = ROPE_BASE ** (-jnp.arange(ROPE_FREQ, dtype=F32) / ROPE_FREQ)
    ar, ac = row[:, None] * inv, col[:, None] * inv
    cr, sr, cc, sc = jnp.cos(ar), jnp.sin(ar), jnp.cos(ac), jnp.sin(ac)
    return jnp.concatenate([cr, cr, cc, cc, -sr, sr, -sc, sc], axis=1)


def _pad8(x):
    return jnp.pad(x, ((0, 8 - x.shape[0]), (0, 0)))


def kernel(x, c, ctx, c_ctx, ada_w, ada_b, norm_mix, norm_ffn, ffn_w1, ffn_w3, ffn_w2, mla_w_dqkv, mla_g_q_lora, mla_g_kv_lora, mla_w_uq, mla_w_ukv, mla_g_qn, mla_g_kn, mla_w_o, rw_mix, rw_w_r, rw_w_k, rw_w_v, rw_w_o, rw_k_k, rw_k_a, rw_r_k, rw_decay_w0, rw_decay_w1, rw_decay_w2, rw_iclr_a0, rw_iclr_a1, rw_iclr_a2, rw_gate_g1, rw_gate_g2, rw_gn_w, rw_gn_b):
    bsz, seq, d = x.shape
    n_ctx = ctx.shape[1]
    hh = MLA_HEADS

    cc = _pad8(jnp.concatenate([c, c_ctx[None, :]], axis=0))
    mod = _adaln_mod(cc, ada_w, ada_b)
    mod_l = [mod[i, :bsz].reshape(bsz, 6, d) for i in range(2)]
    mod_c = [mod[i, bsz:bsz + 1].reshape(1, 6, d) for i in range(2)]

    swap = np.arange(MLA_ROPE) ^ ROPE_FREQ
    wdq = mla_w_dqkv[0]
    rope0 = MLA_Q_LORA + MLA_KV_LORA
    wd = jnp.concatenate([wdq, wdq[:, rope0 + swap]], axis=1).astype(BF16)
    wuq = mla_w_uq[0].reshape(MLA_Q_LORA, hh, MLA_QK)
    wuq = jnp.concatenate([wuq, wuq[:, :, MLA_NOPE + swap]], axis=2).reshape(MLA_Q_LORA, hh * MLA_QK_PAD).astype(BF16)
    wukv = mla_w_ukv[0].astype(BF16)
    gqn, gkn = mla_g_qn[0], mla_g_kn[0]
    gq = jnp.concatenate([gqn, gqn[MLA_NOPE + swap]])[None, :]
    gk = jnp.concatenate([gkn, gkn[MLA_NOPE + swap]])[None, :]
    proj_w = (wd, mla_g_q_lora[0][None, :], mla_g_kv_lora[0][None, :], wuq, wukv, gq, gk)
    tab_l = _rope_table(seq)
    tab_c = jnp.concatenate([jnp.ones((n_ctx, MLA_ROPE), F32), jnp.zeros((n_ctx, MLA_ROPE), F32)], axis=1)
    gmix0 = norm_mix[0][None, :]
    tm_l = min(512, seq)
    tm_c = min(256, n_ctx)
    q_l, k_l, v_l = _mla_proj(x, mod_l[0], gmix0, proj_w, tab_l, tm_l)
    q_c, k_c, v_c = _mla_proj(ctx, mod_c[0], gmix0, proj_w, tab_c, tm_c)
    o_l = _flash(q_l, k_l, v_l, k_c, v_c, min(512, seq), min(512, seq))
    o_c = _flash(q_c, k_c, v_c, None, None, tm_c, n_ctx)

    w1 = ffn_w1.astype(BF16)
    w3 = ffn_w3.astype(BF16)
    w2 = ffn_w2.astype(BF16)
    tf = 256
    wo0 = mla_w_o[0].astype(BF16)
    x1 = _post_ffn(x, o_l, mod_l[0], wo0, norm_ffn[0][None, :], w1[0], w3[0], w2[0], tm_l, tf)
    ctx1 = _post_ffn(ctx, o_c, mod_c[0], wo0, norm_ffn[0][None, :], w1[0], w3[0], w2[0], tm_c, tf)

    zpad = RW_GATE_PAD - RW_GATE_LORA
    dw2 = rw_decay_w2[0]
    ia2 = rw_iclr_a2[0]
    zero = jnp.zeros_like(dw2[0])
    feat_w = (
        rw_w_r[0].astype(BF16), rw_w_k[0].astype(BF16), rw_w_v[0].astype(BF16),
        jnp.concatenate([rw_decay_w1[0, 0], rw_decay_w1[0, 1]], axis=1).astype(BF16),
        jnp.concatenate([jnp.concatenate([dw2[0], zero], 1), jnp.concatenate([zero, dw2[1]], 1)], 0).astype(BF16),
        jnp.concatenate([rw_iclr_a1[0, 0], rw_iclr_a1[0, 1]], axis=1).astype(BF16),
        jnp.concatenate([jnp.concatenate([ia2[0], zero], 1), jnp.concatenate([zero, ia2[1]], 1)], 0).astype(BF16),
        jnp.pad(rw_gate_g1[0], ((0, 0), (0, zpad))).astype(BF16),
        jnp.pad(rw_gate_g2[0], ((0, zpad), (0, 0))).astype(BF16),
    )
    head_of = np.arange(d) // RW_HEAD
    hsum = jnp.asarray(head_of[:, None] == head_of[None, :], BF16)
    vec = _pad8(jnp.stack([rw_k_k[0], rw_k_a[0], rw_decay_w0[0, 0], rw_decay_w0[0, 1],
                           rw_iclr_a0[0, 0], rw_iclr_a0[0, 1]]))
    gmix1 = norm_mix[1][None, :]
    tmf = 256
    r_l, v_l2, g_l, kk_l, kd_l, lw_l, b_l = _rwkv_feat(x1, mod_l[1], gmix1, rw_mix[0], vec, feat_w + (hsum,), min(tmf, seq))
    r_c, v_c2, _, kk_c, kd_c, lw_c, b_c = _rwkv_feat(ctx1, mod_c[1], gmix1, rw_mix[0], vec, feat_w + (hsum,), min(tmf, n_ctx))
    nh = d // RW_HEAD
    s_zero = jnp.zeros((2, bsz, nh, RW_HEAD, RW_HEAD), F32)
    _, s_ctx = _wkv(r_c, v_c2, kk_c, kd_c, lw_c, b_c, s_zero)
    y, _ = _wkv(r_l, v_l2, kk_l, kd_l, lw_l, b_l, s_ctx)
    ovec = _pad8(jnp.stack([rw_r_k[0].reshape(-1), rw_gn_w[0], rw_gn_b[0]]))
    z = _rwkv_out(y, r_l, kd_l, v_l2, g_l, ovec, hsum, min(256, seq))
    return _post_ffn(x1, z, mod_l[1], rw_w_o[0].astype(BF16), norm_ffn[1][None, :], w1[1], w3[1], w2[1], tm_l, tf)
```

```python
import functools

import jax
import jax.numpy as jnp
import numpy as np
from jax import lax
from jax.experimental import pallas as pl
from jax.experimental.pallas import tpu as pltpu

F32 = jnp.float32
BF16 = jnp.bfloat16
HIGHEST = lax.Precision.HIGHEST

EPS = 1e-6
GRID_W = 64
MLA_HEADS = 8
MLA_Q_LORA = 384
MLA_KV_LORA = 256
MLA_NOPE = 128
MLA_ROPE = 64
MLA_V = 128
MLA_QK = MLA_NOPE + MLA_ROPE
MLA_QK_PAD = 256
MLA_V_PAD = 256
Q_SCALE = float(np.log2(np.e)) * MLA_QK ** -0.5
ROPE_FREQ = MLA_ROPE // 4
ROPE_BASE = 10000.0
RW_HEAD = 64
RW_GATE_LORA = 160
RW_GATE_PAD = 256
RW_GN_EPS = 64e-5
WKV_CHUNK = 64
FLASH_ROWS = 256

VMEM_LIMIT_BYTES = 56 * 1024 * 1024


def _params(sem):
    return pltpu.CompilerParams(dimension_semantics=sem, vmem_limit_bytes=VMEM_LIMIT_BYTES)


def _dot(a, b):
    return jnp.dot(a, b, preferred_element_type=F32)


def _dot_nt(a, b):
    return lax.dot_general(a, b, (((1,), (1,)), ((), ())), preferred_element_type=F32)


def _dot_tn(a, b):
    return lax.dot_general(a, b, (((0,), (0,)), ((), ())), preferred_element_type=F32)


def _norm_mod(x, g, shift, scale):
    ms = jnp.mean(x * x, axis=-1, keepdims=True)
    return (x * lax.rsqrt(ms + EPS) * g) * (1.0 + scale) + shift


def _rms(x, g):
    ms = jnp.mean(x * x, axis=-1, keepdims=True)
    return x * lax.rsqrt(ms + EPS) * g


def _full(shape):
    n = len(shape)
    return pl.BlockSpec(shape, lambda *_: (0,) * n)


def _mod_kernel(cc_ref, w_ref, b_ref, o_ref):
    cc = cc_ref[...]
    s = cc * jax.nn.sigmoid(cc)
    o_ref[0] = jnp.dot(s, w_ref[0], precision=HIGHEST, preferred_element_type=F32) + b_ref[0]


def _adaln_mod(cc, ada_w, ada_b):
    depth, d, n = ada_w.shape
    tn = 1536
    rows = cc.shape[0]
    return pl.pallas_call(
        _mod_kernel,
        grid=(depth, n // tn),
        in_specs=[
            pl.BlockSpec((rows, d), lambda i, j: (0, 0)),
            pl.BlockSpec((1, d, tn), lambda i, j: (i, 0, j)),
            pl.BlockSpec((1, 1, tn), lambda i, j: (i, 0, j)),
        ],
        out_specs=pl.BlockSpec((1, rows, tn), lambda i, j: (i, 0, j)),
        out_shape=jax.ShapeDtypeStruct((depth, rows, n), F32),
        compiler_params=_params(("parallel", "parallel")),
        name="adaln_mod",
    )(cc, ada_w, ada_b.reshape(depth, 1, n))


def _mla_proj_kernel(x_ref, mod_ref, gmix_ref, wd_ref, glq_ref, glkv_ref, wuq_ref, wukv_ref,
                     gq_ref, gk_ref, tab_ref, q_ref, k_ref, v_ref):
    mod = mod_ref[0]
    h = _norm_mod(x_ref[0], gmix_ref[...], mod[0:1], mod[1:2])
    down = _dot(h.astype(BF16), wd_ref[...])
    cq = _rms(down[:, :MLA_Q_LORA], glq_ref[...]).astype(BF16)
    ckv = _rms(down[:, MLA_Q_LORA:MLA_Q_LORA + MLA_KV_LORA], glkv_ref[...]).astype(BF16)
    kr = down[:, MLA_Q_LORA + MLA_KV_LORA:]
    tab = tab_ref[...]
    lo = lax.broadcasted_iota(jnp.int32, (1, 128), 1) < MLA_ROPE
    gq = gq_ref[...]
    gk = gk_ref[...]

    def rope(t):
        p = t * tab
        return jnp.where(lo, p + pltpu.roll(p, MLA_ROPE, 1), 0.0)

    kr_ss = jnp.sum(jnp.where(lo, kr * kr, 0.0), axis=-1, keepdims=True)
    k_rot = rope(kr * gk[:, 128:])
    ones_col = (lax.broadcasted_iota(jnp.int32, (x_ref.shape[1], 128), 1) == 0).astype(BF16)
    for hd in range(MLA_HEADS):
        c0 = hd * MLA_QK_PAD
        qh = _dot(cq, wuq_ref[:, c0:c0 + MLA_QK_PAD])
        qn, qr = qh[:, :128], qh[:, 128:]
        ssq = jnp.sum(qn * qn, axis=-1, keepdims=True) + jnp.sum(jnp.where(lo, qr * qr, 0.0), axis=-1, keepdims=True)
        rq = lax.rsqrt(ssq * (1.0 / MLA_QK) + EPS) * Q_SCALE
        q_ref[0, hd, :, :128] = (qn * gq[:, :128] * rq).astype(BF16)
        q_ref[0, hd, :, 128:] = (rope(qr * gq[:, 128:]) * rq).astype(BF16)
        kvh = _dot(ckv, wukv_ref[:, c0:c0 + MLA_QK_PAD])
        kn = kvh[:, :128]
        rk = lax.rsqrt((jnp.sum(kn * kn, axis=-1, keepdims=True) + kr_ss) * (1.0 / MLA_QK) + EPS)
        k_ref[0, hd, :, :128] = (kn * gk[:, :128] * rk).astype(BF16)
        k_ref[0, hd, :, 128:] = (k_rot * rk).astype(BF16)
        v_ref[0, hd, :, :MLA_V] = kvh[:, 128:].astype(BF16)
        v_ref[0, hd, :, MLA_V:] = ones_col


def _mla_proj(x, mod, gmix, wts, tab, tm):
    b, l, d = x.shape
    hh = MLA_HEADS
    per_batch_mod = mod.shape[0] > 1
    mod_map = (lambda bi, i: (bi, 0, 0)) if per_batch_mod else (lambda bi, i: (0, 0, 0))
    wd, glq, glkv, wuq, wukv, gq, gk = wts
    return pl.pallas_call(
        _mla_proj_kernel,
        grid=(b, l // tm),
        in_specs=[
            pl.BlockSpec((1, tm, d), lambda bi, i: (bi, i, 0)),
            pl.BlockSpec((1, 6, d), mod_map),
            _full(gmix.shape), _full(wd.shape), _full(glq.shape), _full(glkv.shape),
            _full(wuq.shape), _full(wukv.shape), _full(gq.shape), _full(gk.shape),
            pl.BlockSpec((tm, 128), lambda bi, i: (i, 0)),
        ],
        out_specs=[
            pl.BlockSpec((1, hh, tm, MLA_QK_PAD), lambda bi, i: (bi, 0, i, 0)),
            pl.BlockSpec((1, hh, tm, MLA_QK_PAD), lambda bi, i: (bi, 0, i, 0)),
            pl.BlockSpec((1, hh, tm, MLA_V_PAD), lambda bi, i: (bi, 0, i, 0)),
        ],
        out_shape=[
            jax.ShapeDtypeStruct((b, hh, l, MLA_QK_PAD), BF16),
            jax.ShapeDtypeStruct((b, hh, l, MLA_QK_PAD), BF16),
            jax.ShapeDtypeStruct((b, hh, l, MLA_V_PAD), BF16),
        ],
        compiler_params=_params(("parallel", "parallel")),
        name="mla_proj",
    )(x, mod, gmix, wd, glq, glkv, wuq, wukv, gq, gk, tab)


def _flash_kernel(*refs, has_ctx, n_sub):
    if has_ctx:
        q_ref, k_ref, v_ref, kc_ref, vc_ref, o_ref, m_sc, acc_sc = refs
    else:
        q_ref, k_ref, v_ref, o_ref, m_sc, acc_sc = refs
    j = pl.program_id(3)
    tr = min(FLASH_ROWS, q_ref.shape[2])
    n_row = q_ref.shape[2] // tr

    def attend(kv_ref, vv_ref, n_chunk):
        ts = kv_ref.shape[2] // n_chunk
        pairs = [(c, r) for c in range(n_chunk) for r in range(n_row)]
        m_run = [m_sc[r * tr:(r + 1) * tr] for r in range(n_row)]
        s_of, p_of = {}, {}
        for t in range(len(pairs) + 2):
            if t < len(pairs):
                c, r = pairs[t]
                s_of[t] = _dot_nt(q_ref[0, 0, r * tr:(r + 1) * tr], kv_ref[0, 0, c * ts:(c + 1) * ts])
            if 0 <= t - 1 < len(pairs):
                c, r = pairs[t - 1]
                s = s_of.pop(t - 1)
                m_new = jnp.maximum(m_run[r], jnp.max(s, axis=-1, keepdims=True))
                p_of[t - 1] = (jnp.exp2(s - m_new).astype(BF16), jnp.exp2(m_run[r] - m_new))
                m_run[r] = m_new
            if 0 <= t - 2 < len(pairs):
                c, r = pairs[t - 2]
                p, alpha = p_of.pop(t - 2)
                rows = slice(r * tr, (r + 1) * tr)
                acc_sc[rows] = alpha * acc_sc[rows] + _dot(p, vv_ref[0, 0, c * ts:(c + 1) * ts])
        for r in range(n_row):
            m_sc[r * tr:(r + 1) * tr] = m_run[r]

    @pl.when(j == 0)
    def _():
        m_sc[...] = jnp.full(m_sc.shape, -1e30, F32)
        acc_sc[...] = jnp.zeros(acc_sc.shape, F32)
        if has_ctx:
            attend(kc_ref, vc_ref, 1)

    attend(k_ref, v_ref, n_sub)

    @pl.when(j == pl.num_programs(3) - 1)
    def _():
        acc = acc_sc[...]
        o_ref[0] = (acc[:, :MLA_V] / acc[:, MLA_V:MLA_V + 1]).astype(o_ref.dtype)


def _flash(q, k, v, kc, vc, tq, tk, n_sub):
    b, hh, sq, dq = q.shape
    lk = k.shape[2]
    dv = v.shape[3]
    has_ctx = kc is not None
    in_specs = [
        pl.BlockSpec((1, 1, tq, dq), lambda bi, h, i, j: (bi, h, i, 0)),
        pl.BlockSpec((1, 1, tk, dq), lambda bi, h, i, j: (bi, h, j, 0)),
        pl.BlockSpec((1, 1, tk, dv), lambda bi, h, i, j: (bi, h, j, 0)),
    ]
    args = [q, k, v]
    if has_ctx:
        lc = kc.shape[2]
        in_specs += [
            pl.BlockSpec((1, 1, lc, dq), lambda bi, h, i, j: (bi, h, 0, 0)),
            pl.BlockSpec((1, 1, lc, dv), lambda bi, h, i, j: (bi, h, 0, 0)),
        ]
        args += [kc, vc]
    return pl.pallas_call(
        functools.partial(_flash_kernel, has_ctx=has_ctx, n_sub=n_sub),
        grid=(b, hh, sq // tq, lk // tk),
        in_specs=in_specs,
        out_specs=pl.BlockSpec((1, tq, MLA_V), lambda bi, h, i, j: (bi, i, h)),
        out_shape=jax.ShapeDtypeStruct((b, sq, hh * MLA_V), BF16),
        scratch_shapes=[pltpu.VMEM((tq, 1), F32), pltpu.VMEM((tq, dv), F32)],
        compiler_params=_params(("parallel", "parallel", "parallel", "arbitrary")),
        name="mla_flash_ctx" if has_ctx else "mla_flash",
    )(*args)


def _post_ffn_kernel(x_ref, z_ref, mod_ref, wo_ref, gffn_ref, w1_ref, w3_ref, w2_ref, o_ref,
                     x1_sc, h_sc, acc_sc):
    f = pl.program_id(2)
    mod = mod_ref[0]

    @pl.when(f == 0)
    def _():
        x1 = x_ref[0] + mod[2:3] * _dot(z_ref[0], wo_ref[...])
        x1_sc[...] = x1
        h_sc[...] = _norm_mod(x1, gffn_ref[...], mod[3:4], mod[4:5]).astype(BF16)
        acc_sc[...] = jnp.zeros(acc_sc.shape, F32)

    h = h_sc[...]
    a = _dot(h, w1_ref[...])
    g = (a * jax.nn.sigmoid(a)) * _dot(h, w3_ref[...])
    acc_sc[...] += _dot(g.astype(BF16), w2_ref[...])

    @pl.when(f == pl.num_programs(2) - 1)
    def _():
        o_ref[0] = x1_sc[...] + mod[5:6] * acc_sc[...]


def _post_ffn(x, z, mod, wo, gffn, w1, w3, w2, tm, tf):
    b, l, d = x.shape
    dff = w1.shape[1]
    per_batch_mod = mod.shape[0] > 1
    mod_map = (lambda bi, i, f: (bi, 0, 0)) if per_batch_mod else (lambda bi, i, f: (0, 0, 0))
    return pl.pallas_call(
        _post_ffn_kernel,
        grid=(b, l // tm, dff // tf),
        in_specs=[
            pl.BlockSpec((1, tm, d), lambda bi, i, f: (bi, i, 0)),
            pl.BlockSpec((1, tm, d), lambda bi, i, f: (bi, i, 0)),
            pl.BlockSpec((1, 6, d), mod_map),
            pl.BlockSpec((d, d), lambda bi, i, f: (0, 0)),
            pl.BlockSpec((1, d), lambda bi, i, f: (0, 0)),
            pl.BlockSpec((d, tf), lambda bi, i, f: (0, f)),
            pl.BlockSpec((d, tf), lambda bi, i, f: (0, f)),
            pl.BlockSpec((tf, d), lambda bi, i, f: (f, 0)),
        ],
        out_specs=pl.BlockSpec((1, tm, d), lambda bi, i, f: (bi, i, 0)),
        out_shape=jax.ShapeDtypeStruct((b, l, d), F32),
        scratch_shapes=[pltpu.VMEM((tm, d), F32), pltpu.VMEM((tm, d), BF16), pltpu.VMEM((tm, d), F32)],
        compiler_params=_params(("parallel", "parallel", "arbitrary")),
        name="post_ffn",
    )(x, z, mod, wo, gffn, w1, w3, w2)


def _rwkv_feat_kernel(x_ref, xp_ref, xn_ref, mod_ref, gmix_ref, mix_ref, vec_ref, wr_ref, wk_ref, wv_ref,
                      dw1_ref, dw2_ref, ia1_ref, ia2_ref, g1_ref, g2_ref, hsum_ref,
                      r_ref, v_ref, g_ref, kk_ref, kd_ref, lw_ref, b_ref):
    i = pl.program_id(1)
    nt = pl.num_programs(1)
    mod = mod_ref[0]
    gmix = gmix_ref[...]
    tm, d = x_ref.shape[1], x_ref.shape[2]
    h = _norm_mod(x_ref[0], gmix, mod[0:1], mod[1:2])
    hp = _norm_mod(xp_ref[0], gmix, mod[0:1], mod[1:2])[7:8]
    hn = _norm_mod(xn_ref[0], gmix, mod[0:1], mod[1:2])[0:1]
    hp = jnp.where(i > 0, hp, 0.0)
    hn = jnp.where(i < nt - 1, hn, 0.0)
    row = lax.broadcasted_iota(jnp.int32, (tm, 1), 0)
    prev = jnp.where(row == 0, hp, pltpu.roll(h, 1, 0))
    nxt = jnp.where(row == tm - 1, hn, pltpu.roll(h, tm - 1, 0))
    xx = 0.5 * (prev + nxt) - h
    mix = mix_ref[...]
    vec = vec_ref[...]

    def mixed(jj):
        return (h + xx * mix[jj:jj + 1]).astype(BF16)

    r_ref[0] = _dot(mixed(0), wr_ref[...])
    k = _dot(mixed(2), wk_ref[...])
    v_ref[0] = _dot(mixed(3), wv_ref[...])
    g_ref[0] = _dot(jax.nn.sigmoid(_dot(mixed(5), g1_ref[...])).astype(BF16), g2_ref[...])
    dd = _dot(jnp.tanh(_dot(mixed(1), dw1_ref[...])).astype(BF16), dw2_ref[...])
    aa = _dot(_dot(mixed(4), ia1_ref[...]).astype(BF16), ia2_ref[...])
    kkr = k * vec[0:1]
    kk = kkr * lax.rsqrt(_dot((kkr * kkr).astype(BF16), hsum_ref[...]) + 1e-12)
    kk_ref[0] = kk
    for dr in range(2):
        z = -(vec[2 + dr:3 + dr] + dd[:, dr * d:(dr + 1) * d])
        softplus = jnp.maximum(z, 0.0) + jnp.log(1.0 + jnp.exp(-jnp.abs(z)))
        lw_ref[dr, 0] = -jnp.exp(-softplus - 0.5)
        a = jax.nn.sigmoid(vec[4 + dr:5 + dr] + aa[:, dr * d:(dr + 1) * d])
        kd_ref[dr, 0] = k * (1.0 + (a - 1.0) * vec[1:2])
        b_ref[dr, 0] = kk * a


def _rwkv_feat(x, mod, gmix, mix, vec, wts, tm):
    b, l, d = x.shape
    per_batch_mod = mod.shape[0] > 1
    mod_map = (lambda bi, i: (bi, 0, 0)) if per_batch_mod else (lambda bi, i: (0, 0, 0))
    t8 = tm // 8
    last8 = l // 8 - 1
    tok = pl.BlockSpec((1, tm, d), lambda bi, i: (bi, i, 0))
    tok2 = pl.BlockSpec((2, 1, tm, d), lambda bi, i: (0, bi, i, 0))
    one = jax.ShapeDtypeStruct((b, l, d), F32)
    two = jax.ShapeDtypeStruct((2, b, l, d), F32)
    return pl.pallas_call(
        _rwkv_feat_kernel,
        grid=(b, l // tm),
        in_specs=[
            tok,
            pl.BlockSpec((1, 8, d), lambda bi, i: (bi, jnp.maximum(i * t8 - 1, 0), 0)),
            pl.BlockSpec((1, 8, d), lambda bi, i: (bi, jnp.minimum((i + 1) * t8, last8), 0)),
            pl.BlockSpec((1, 6, d), mod_map),
            _full(gmix.shape), _full(mix.shape), _full(vec.shape),
        ] + [_full(w.shape) for w in wts],
        out_specs=[tok, tok, tok, tok, tok2, tok2, tok2],
        out_shape=[one, one, one, one, two, two, two],
        compiler_params=_params(("parallel", "parallel")),
        name="rwkv_feat",
    )(x, x, x, mod, gmix, mix, vec, *wts)


def _wkv_kernel(r_ref, v_ref, kk_ref, kd_ref, lw_ref, b_ref, s0_ref, y_ref, sT_ref, st_sc):
    dr = pl.program_id(0)
    s = pl.program_id(2)
    c = WKV_CHUNK
    n = RW_HEAD
    d = r_ref.shape[2]
    nh = d // n

    @pl.when(s == 0)
    def _():
        st_sc[...] = s0_ref[0, 0]

    sign = 1 - 2 * dr
    tt = lax.broadcasted_iota(jnp.int32, (c, c), 0)
    ss = lax.broadcasted_iota(jnp.int32, (c, c), 1)
    ahead = (tt - ss) * sign
    incl = ahead >= 0
    strict = ahead > 0
    eye = (tt == ss).astype(F32)

    lw = lw_ref[0, 0]
    cum = jnp.dot(incl.astype(F32), lw, precision=HIGHEST, preferred_element_type=F32)
    e_pos = jnp.exp(cum)
    e_neg = jnp.exp(-cum)
    last = jnp.where(dr == 0, c - 1, 0)
    rowsel = lax.broadcasted_iota(jnp.int32, (c, 1), 0) == last
    w_end = jnp.sum(jnp.where(rowsel, e_pos, 0.0), axis=0, keepdims=True)
    rt = r_ref[0] * e_pos
    at = -kk_ref[0] * jnp.exp(cum - lw)
    bt = b_ref[0, 0] * e_neg
    kt = kd_ref[0, 0] * e_neg
    bw = bt * w_end
    kw = kt * w_end
    v = v_ref[0]

    heads = range(nh)
    sls = [slice(hd * n, (hd + 1) * n) for hd in heads]
    s0 = [st_sc[hd] for hd in heads]
    ar = [jnp.concatenate([at[:, sl], rt[:, sl]], axis=0).astype(BF16) for sl in sls]
    bk = [jnp.concatenate([bt[:, sl], kt[:, sl]], axis=0).astype(BF16) for sl in sls]
    vb = [v[:, sl].astype(BF16) for sl in sls]
    p = [_dot_nt(ar[h], bk[h]) for h in heads]
    ars = [_dot_nt(ar[h], s0[h].astype(BF16)) for h in heads]
    l_ab = [jnp.where(strict, p[h][:c, :c], 0.0) for h in heads]
    l_b = [l_ab[h].astype(BF16) for h in heads]
    lakv = [_dot(jnp.where(strict, p[h][:c, c:], 0.0).astype(BF16), vb[h]) for h in heads]
    m = [jnp.concatenate([jnp.where(incl, p[h][c:, :c], 0.0), jnp.where(incl, p[h][c:, c:], 0.0)],
                         axis=1).astype(BF16) for h in heads]
    tinv = [eye + l_ab[h] for h in heads]
    lp = [_dot(l_b[h], l_b[h]) for h in heads]
    for step in range(5):
        lp_b = [lp[h].astype(BF16) for h in heads]
        upd = [_dot(lp_b[h], tinv[h].astype(BF16)) for h in heads]
        if step < 4:
            lp = [_dot(lp_b[h], lp_b[h]) for h in heads]
        tinv = [tinv[h] + upd[h] for h in heads]
    x = [(ars[h][:c] + lakv[h]).astype(BF16) for h in heads]
    u = [_dot(tinv[h].astype(BF16), x[h]) for h in heads]
    uv = [jnp.concatenate([u[h].astype(BF16), vb[h]], axis=0) for h in heads]
    yh = [ars[h][c:] + _dot(m[h], uv[h]) for h in heads]
    bkw = [jnp.concatenate([bw[:, sl], kw[:, sl]], axis=0).astype(BF16) for sl in sls]
    s1 = [s0[h] * w_end[:, sls[h]] + _dot_tn(uv[h], bkw[h]) for h in heads]
    for h in heads:
        y_ref[0, 0, :, sls[h]] = yh[h]
        st_sc[h] = s1[h]

    @pl.when(s == pl.num_programs(2) - 1)
    def _():
        sT_ref[0, 0] = st_sc[...]


def _wkv(r, v, kk, kd, lw, bb, s0):
    b, l, d = r.shape
    c = WKV_CHUNK
    nc = l // c
    nh = d // RW_HEAD

    def cidx(dr, s):
        return s + dr * (nc - 1 - 2 * s)

    tok = pl.BlockSpec((1, c, d), lambda dr, bi, s: (bi, cidx(dr, s), 0))
    tok2 = pl.BlockSpec((1, 1, c, d), lambda dr, bi, s: (dr, bi, cidx(dr, s), 0))
    st = pl.BlockSpec((1, 1, nh, RW_HEAD, RW_HEAD), lambda dr, bi, s: (dr, bi, 0, 0, 0))
    return pl.pallas_call(
        _wkv_kernel,
        grid=(2, b, nc),
        in_specs=[tok, tok, tok, tok2, tok2, tok2, st],
        out_specs=[tok2, st],
        out_shape=[jax.ShapeDtypeStruct((2, b, l, d), F32),
                   jax.ShapeDtypeStruct((2, b, nh, RW_HEAD, RW_HEAD), F32)],
        scratch_shapes=[pltpu.VMEM((nh, RW_HEAD, RW_HEAD), F32)],
        compiler_params=_params(("parallel", "parallel", "arbitrary")),
        name="wkv7",
    )(r, v, kk, kd, lw, bb, s0)


def _rwkv_out_kernel(y_ref, r_ref, kd_ref, v_ref, g_ref, vec_ref, hsum_ref, z_ref):
    vec = vec_ref[...]
    hsum = hsum_ref[...]
    inv_n = 1.0 / RW_HEAD
    y = y_ref[0, 0] + y_ref[1, 0]
    mu = _dot(y.astype(BF16), hsum) * inv_n
    yc = y - mu
    var = _dot((yc * yc).astype(BF16), hsum) * inv_n
    yn = yc * lax.rsqrt(var + RW_GN_EPS) * vec[1:2] + vec[2:3]
    rk = r_ref[0] * (kd_ref[0, 0] + kd_ref[1, 0]) * vec[0:1]
    bonus = _dot(rk.astype(BF16), hsum) * v_ref[0]
    z_ref[0] = ((yn + bonus) * g_ref[0]).astype(z_ref.dtype)


def _rwkv_out(y, r, kd, v, g, vec, hsum, tm):
    b, l, d = r.shape
    tok = pl.BlockSpec((1, tm, d), lambda bi, i: (bi, i, 0))
    tok2 = pl.BlockSpec((2, 1, tm, d), lambda bi, i: (0, bi, i, 0))
    return pl.pallas_call(
        _rwkv_out_kernel,
        grid=(b, l // tm),
        in_specs=[tok2, tok, tok2, tok, tok, _full(vec.shape), _full(hsum.shape)],
        out_specs=tok,
        out_shape=jax.ShapeDtypeStruct((b, l, d), BF16),
        compiler_params=_params(("parallel", "parallel")),
        name="rwkv_out",
    )(y, r, kd, v, g, vec, hsum)


def _rope_table(n_tokens):
    rows = n_tokens // GRID_W
    row = jnp.broadcast_to(jnp.arange(rows, dtype=F32)[:, None], (rows, GRID_W)).reshape(-1)
    col = jnp.broadcast_to(jnp.arange(GRID_W, dtype=F32)[None, :], (rows, GRID_W)).reshape(-1)
    inv = ROPE_BASE ** (-jnp.arange(ROPE_FREQ, dtype=F32) / ROPE_FREQ)
    ar, ac = row[:, None] * inv, col[:, None] * inv
    cr, sr, cc, sc = jnp.cos(ar), jnp.sin(ar), jnp.cos(ac), jnp.sin(ac)
    return jnp.concatenate([cr, cr, cc, cc, -sr, sr, -sc, sc], axis=1)


def _pad8(x):
    return jnp.pad(x, ((0, 8 - x.shape[0]), (0, 0)))


def kernel(x, c, ctx, c_ctx, ada_w, ada_b, norm_mix, norm_ffn, ffn_w1, ffn_w3, ffn_w2, mla_w_dqkv, mla_g_q_lora, mla_g_kv_lora, mla_w_uq, mla_w_ukv, mla_g_qn, mla_g_kn, mla_w_o, rw_mix, rw_w_r, rw_w_k, rw_w_v, rw_w_o, rw_k_k, rw_k_a, rw_r_k, rw_decay_w0, rw_decay_w1, rw_decay_w2, rw_iclr_a0, rw_iclr_a1, rw_iclr_a2, rw_gate_g1, rw_gate_g2, rw_gn_w, rw_gn_b):
    bsz, seq, d = x.shape
    n_ctx = ctx.shape[1]
    hh = MLA_HEADS

    cc = _pad8(jnp.concatenate([c, c_ctx[None, :]], axis=0))
    mod = _adaln_mod(cc, ada_w, ada_b)
    mod_l = [mod[i, :bsz].reshape(bsz, 6, d) for i in range(2)]
    mod_c = [mod[i, bsz:bsz + 1].reshape(1, 6, d) for i in range(2)]

    swap = np.arange(MLA_ROPE) ^ ROPE_FREQ
    wdq = mla_w_dqkv[0]
    rope0 = MLA_Q_LORA + MLA_KV_LORA
    wd = jnp.concatenate([wdq, wdq[:, rope0 + swap]], axis=1).astype(BF16)
    wuq = mla_w_uq[0].reshape(MLA_Q_LORA, hh, MLA_QK)
    wuq = jnp.concatenate([wuq, wuq[:, :, MLA_NOPE + swap]], axis=2).reshape(MLA_Q_LORA, hh * MLA_QK_PAD).astype(BF16)
    wukv = mla_w_ukv[0].astype(BF16)
    gqn, gkn = mla_g_qn[0], mla_g_kn[0]
    gq = jnp.concatenate([gqn, gqn[MLA_NOPE + swap]])[None, :]
    gk = jnp.concatenate([gkn, gkn[MLA_NOPE + swap]])[None, :]
    proj_w = (wd, mla_g_q_lora[0][None, :], mla_g_kv_lora[0][None, :], wuq, wukv, gq, gk)
    tab_l = _rope_table(seq)
    tab_c = jnp.concatenate([jnp.ones((n_ctx, MLA_ROPE), F32), jnp.zeros((n_ctx, MLA_ROPE), F32)], axis=1)
    gmix0 = norm_mix[0][None, :]
    tm_l = min(512, seq)
    tm_c = min(256, n_ctx)
    q_l, k_l, v_l = _mla_proj(x, mod_l[0], gmix0, proj_w, tab_l, tm_l)
    q_c, k_c, v_c = _mla_proj(ctx, mod_c[0], gmix0, proj_w, tab_c, tm_c)
    tk_l = min(2048, seq)
    o_l = _flash(q_l, k_l, v_l, k_c, v_c, min(1024, seq), tk_l, max(tk_l // 512, 1))
    o_c = _flash(q_c, k_c, v_c, None, None, tm_c, n_ctx, 1)

    w1 = ffn_w1.astype(BF16)
    w3 = ffn_w3.astype(BF16)
    w2 = ffn_w2.astype(BF16)
    tf = 256
    wo0 = mla_w_o[0].astype(BF16)
    x1 = _post_ffn(x, o_l, mod_l[0], wo0, norm_ffn[0][None, :], w1[0], w3[0], w2[0], tm_l, tf)
    ctx1 = _post_ffn(ctx, o_c, mod_c[0], wo0, norm_ffn[0][None, :], w1[0], w3[0], w2[0], tm_c, tf)

    zpad = RW_GATE_PAD - RW_GATE_LORA
    dw2 = rw_decay_w2[0]
    ia2 = rw_iclr_a2[0]
    zero = jnp.zeros_like(dw2[0])
    feat_w = (
        rw_w_r[0].astype(BF16), rw_w_k[0].astype(BF16), rw_w_v[0].astype(BF16),
        jnp.concatenate([rw_decay_w1[0, 0], rw_decay_w1[0, 1]], axis=1).astype(BF16),
        jnp.concatenate([jnp.concatenate([dw2[0], zero], 1), jnp.concatenate([zero, dw2[1]], 1)], 0).astype(BF16),
        jnp.concatenate([rw_iclr_a1[0, 0], rw_iclr_a1[0, 1]], axis=1).astype(BF16),
        jnp.concatenate([jnp.concatenate([ia2[0], zero], 1), jnp.concatenate([zero, ia2[1]], 1)], 0).astype(BF16),
        jnp.pad(rw_gate_g1[0], ((0, 0), (0, zpad))).astype(BF16),
        jnp.pad(rw_gate_g2[0], ((0, zpad), (0, 0))).astype(BF16),
    )
    head_of = np.arange(d) // RW_HEAD
    hsum = jnp.asarray(head_of[:, None] == head_of[None, :], BF16)
    vec = _pad8(jnp.stack([rw_k_k[0], rw_k_a[0], rw_decay_w0[0, 0], rw_decay_w0[0, 1],
                           rw_iclr_a0[0, 0], rw_iclr_a0[0, 1]]))
    gmix1 = norm_mix[1][None, :]
    tmf = 256
    r_l, v_l2, g_l, kk_l, kd_l, lw_l, b_l = _rwkv_feat(x1, mod_l[1], gmix1, rw_mix[0], vec, feat_w + (hsum,), min(tmf, seq))
    r_c, v_c2, _, kk_c, kd_c, lw_c, b_c = _rwkv_feat(ctx1, mod_c[1], gmix1, rw_mix[0], vec, feat_w + (hsum,), min(tmf, n_ctx))
    nh = d // RW_HEAD
    s_zero = jnp.zeros((2, bsz, nh, RW_HEAD, RW_HEAD), F32)
    _, s_ctx = _wkv(r_c, v_c2, kk_c, kd_c, lw_c, b_c, s_zero)
    y, _ = _wkv(r_l, v_l2, kk_l, kd_l, lw_l, b_l, s_ctx)
    ovec = _pad8(jnp.stack([rw_r_k[0].reshape(-1), rw_gn_w[0], rw_gn_b[0]]))
    z = _rwkv_out(y, r_l, kd_l, v_l2, g_l, ovec, hsum, min(256, seq))
    return _post_ffn(x1, z, mod_l[1], rw_w_o[0].astype(BF16), norm_ffn[1][None, :], w1[1], w3[1], w2[1], tm_l, tf)
```

```python
import functools

import jax
import jax.numpy as jnp
import numpy as np
from jax import lax
from jax.experimental import pallas as pl
from jax.experimental.pallas import tpu as pltpu

F32 = jnp.float32
BF16 = jnp.bfloat16
HIGHEST = lax.Precision.HIGHEST

EPS = 1e-6
GRID_W = 64
MLA_HEADS = 8
MLA_Q_LORA = 384
MLA_KV_LORA = 256
MLA_NOPE = 128
MLA_ROPE = 64
MLA_V = 128
MLA_QK = MLA_NOPE + MLA_ROPE
MLA_QK_PAD = 256
MLA_V_PAD = 256
Q_SCALE = float(np.log2(np.e)) * MLA_QK ** -0.5
ROPE_FREQ = MLA_ROPE // 4
ROPE_BASE = 10000.0
RW_HEAD = 64
RW_GATE_LORA = 160
RW_GATE_PAD = 256
RW_GN_EPS = 64e-5
WKV_CHUNK = 64
WKV_GROUP = 4
FLASH_ROWS = 256
FLASH_KEYS = 4096

VMEM_LIMIT_BYTES = 56 * 1024 * 1024


def _params(sem):
    return pltpu.CompilerParams(dimension_semantics=sem, vmem_limit_bytes=VMEM_LIMIT_BYTES)


def _dot(a, b):
    return jnp.dot(a, b, preferred_element_type=F32)


def _dot_nt(a, b):
    return lax.dot_general(a, b, (((1,), (1,)), ((), ())), preferred_element_type=F32)


def _dot_tn(a, b):
    return lax.dot_general(a, b, (((0,), (0,)), ((), ())), preferred_element_type=F32)


def _norm_mod(x, g, shift, scale):
    ms = jnp.mean(x * x, axis=-1, keepdims=True)
    return (x * lax.rsqrt(ms + EPS) * g) * (1.0 + scale) + shift


def _rms(x, g):
    ms = jnp.mean(x * x, axis=-1, keepdims=True)
    return x * lax.rsqrt(ms + EPS) * g


def _full(shape):
    n = len(shape)
    return pl.BlockSpec(shape, lambda *_: (0,) * n)


def _mod_kernel(cc_ref, w_ref, b_ref, o_ref):
    cc = cc_ref[...]
    s = cc * jax.nn.sigmoid(cc)
    o_ref[0] = jnp.dot(s, w_ref[0], precision=HIGHEST, preferred_element_type=F32) + b_ref[0]


def _adaln_mod(cc, ada_w, ada_b):
    depth, d, n = ada_w.shape
    tn = 1536
    rows = cc.shape[0]
    return pl.pallas_call(
        _mod_kernel,
        grid=(depth, n // tn),
        in_specs=[
            pl.BlockSpec((rows, d), lambda i, j: (0, 0)),
            pl.BlockSpec((1, d, tn), lambda i, j: (i, 0, j)),
            pl.BlockSpec((1, 1, tn), lambda i, j: (i, 0, j)),
        ],
        out_specs=pl.BlockSpec((1, rows, tn), lambda i, j: (i, 0, j)),
        out_shape=jax.ShapeDtypeStruct((depth, rows, n), F32),
        compiler_params=_params(("parallel", "parallel")),
        name="adaln_mod",
    )(cc, ada_w, ada_b.reshape(depth, 1, n))


def _mla_proj_kernel(x_ref, mod_ref, gmix_ref, wd_ref, glq_ref, glkv_ref, wuq_ref, wukv_ref,
                     gq_ref, gk_ref, tab_ref, q_ref, k_ref, v_ref):
    mod = mod_ref[0]
    h = _norm_mod(x_ref[0], gmix_ref[...], mod[0:1], mod[1:2])
    down = _dot(h.astype(BF16), wd_ref[...])
    cq = _rms(down[:, :MLA_Q_LORA], glq_ref[...]).astype(BF16)
    ckv = _rms(down[:, MLA_Q_LORA:MLA_Q_LORA + MLA_KV_LORA], glkv_ref[...]).astype(BF16)
    kr = down[:, MLA_Q_LORA + MLA_KV_LORA:]
    tab = tab_ref[...]
    lo = lax.broadcasted_iota(jnp.int32, (1, 128), 1) < MLA_ROPE
    gq = gq_ref[...]
    gk = gk_ref[...]

    def rope(t):
        p = t * tab
        return jnp.where(lo, p + pltpu.roll(p, MLA_ROPE, 1), 0.0)

    kr_ss = jnp.sum(jnp.where(lo, kr * kr, 0.0), axis=-1, keepdims=True)
    k_rot = rope(kr * gk[:, 128:])
    ones_col = (lax.broadcasted_iota(jnp.int32, (x_ref.shape[1], 128), 1) == 0).astype(BF16)
    for hd in range(MLA_HEADS):
        c0 = hd * MLA_QK_PAD
        qh = _dot(cq, wuq_ref[:, c0:c0 + MLA_QK_PAD])
        qn, qr = qh[:, :128], qh[:, 128:]
        ssq = jnp.sum(qn * qn, axis=-1, keepdims=True) + jnp.sum(jnp.where(lo, qr * qr, 0.0), axis=-1, keepdims=True)
        rq = lax.rsqrt(ssq * (1.0 / MLA_QK) + EPS) * Q_SCALE
        q_ref[0, hd, :, :128] = (qn * gq[:, :128] * rq).astype(BF16)
        q_ref[0, hd, :, 128:] = (rope(qr * gq[:, 128:]) * rq).astype(BF16)
        kvh = _dot(ckv, wukv_ref[:, c0:c0 + MLA_QK_PAD])
        kn = kvh[:, :128]
        rk = lax.rsqrt((jnp.sum(kn * kn, axis=-1, keepdims=True) + kr_ss) * (1.0 / MLA_QK) + EPS)
        k_ref[0, hd, :, :128] = (kn * gk[:, :128] * rk).astype(BF16)
        k_ref[0, hd, :, 128:] = (k_rot * rk).astype(BF16)
        v_ref[0, hd, :, :MLA_V] = kvh[:, 128:].astype(BF16)
        v_ref[0, hd, :, MLA_V:] = ones_col


def _mla_proj(x, mod, gmix, wts, tab, tm):
    b, l, d = x.shape
    hh = MLA_HEADS
    per_batch_mod = mod.shape[0] > 1
    mod_map = (lambda bi, i: (bi, 0, 0)) if per_batch_mod else (lambda bi, i: (0, 0, 0))
    wd, glq, glkv, wuq, wukv, gq, gk = wts
    return pl.pallas_call(
        _mla_proj_kernel,
        grid=(b, l // tm),
        in_specs=[
            pl.BlockSpec((1, tm, d), lambda bi, i: (bi, i, 0)),
            pl.BlockSpec((1, 6, d), mod_map),
            _full(gmix.shape), _full(wd.shape), _full(glq.shape), _full(glkv.shape),
            _full(wuq.shape), _full(wukv.shape), _full(gq.shape), _full(gk.shape),
            pl.BlockSpec((tm, 128), lambda bi, i: (i, 0)),
        ],
        out_specs=[
            pl.BlockSpec((1, hh, tm, MLA_QK_PAD), lambda bi, i: (bi, 0, i, 0)),
            pl.BlockSpec((1, hh, tm, MLA_QK_PAD), lambda bi, i: (bi, 0, i, 0)),
            pl.BlockSpec((1, hh, tm, MLA_V_PAD), lambda bi, i: (bi, 0, i, 0)),
        ],
        out_shape=[
            jax.ShapeDtypeStruct((b, hh, l, MLA_QK_PAD), BF16),
            jax.ShapeDtypeStruct((b, hh, l, MLA_QK_PAD), BF16),
            jax.ShapeDtypeStruct((b, hh, l, MLA_V_PAD), BF16),
        ],
        compiler_params=_params(("parallel", "parallel")),
        name="mla_proj",
    )(x, mod, gmix, wd, glq, glkv, wuq, wukv, gq, gk, tab)


def _flash_kernel(*refs, has_ctx, n_sub):
    if has_ctx:
        q_ref, k_ref, v_ref, kc_ref, vc_ref, o_ref, m_sc, acc_sc = refs
    else:
        q_ref, k_ref, v_ref, o_ref, m_sc, acc_sc = refs
    j = pl.program_id(3)
    tr = min(FLASH_ROWS, q_ref.shape[2])
    n_row = q_ref.shape[2] // tr

    def attend(kv_ref, vv_ref, n_chunk):
        ts = kv_ref.shape[2] // n_chunk
        pairs = [(c, r) for c in range(n_chunk) for r in range(n_row)]
        m_run = [m_sc[r * tr:(r + 1) * tr] for r in range(n_row)]
        s_of, p_of = {}, {}
        for t in range(len(pairs) + 2):
            if t < len(pairs):
                c, r = pairs[t]
                s_of[t] = _dot_nt(q_ref[0, 0, r * tr:(r + 1) * tr], kv_ref[0, 0, c * ts:(c + 1) * ts])
            if 0 <= t - 1 < len(pairs):
                c, r = pairs[t - 1]
                s = s_of.pop(t - 1)
                m_new = jnp.maximum(m_run[r], jnp.max(s, axis=-1, keepdims=True))
                p_of[t - 1] = (jnp.exp2(s - m_new).astype(BF16), jnp.exp2(m_run[r] - m_new))
                m_run[r] = m_new
            if 0 <= t - 2 < len(pairs):
                c, r = pairs[t - 2]
                p, alpha = p_of.pop(t - 2)
                rows = slice(r * tr, (r + 1) * tr)
                acc_sc[rows] = alpha * acc_sc[rows] + _dot(p, vv_ref[0, 0, c * ts:(c + 1) * ts])
        for r in range(n_row):
            m_sc[r * tr:(r + 1) * tr] = m_run[r]

    @pl.when(j == 0)
    def _():
        m_sc[...] = jnp.full(m_sc.shape, -1e30, F32)
        acc_sc[...] = jnp.zeros(acc_sc.shape, F32)
        if has_ctx:
            attend(kc_ref, vc_ref, 1)

    attend(k_ref, v_ref, n_sub)

    @pl.when(j == pl.num_programs(3) - 1)
    def _():
        acc = acc_sc[...]
        o_ref[0] = (acc[:, :MLA_V] / acc[:, MLA_V:MLA_V + 1]).astype(o_ref.dtype)


def _flash(q, k, v, kc, vc, tq, tk, n_sub):
    b, hh, sq, dq = q.shape
    lk = k.shape[2]
    dv = v.shape[3]
    has_ctx = kc is not None
    in_specs = [
        pl.BlockSpec((1, 1, tq, dq), lambda bi, h, i, j: (bi, h, i, 0)),
        pl.BlockSpec((1, 1, tk, dq), lambda bi, h, i, j: (bi, h, j, 0)),
        pl.BlockSpec((1, 1, tk, dv), lambda bi, h, i, j: (bi, h, j, 0)),
    ]
    args = [q, k, v]
    if has_ctx:
        lc = kc.shape[2]
        in_specs += [
            pl.BlockSpec((1, 1, lc, dq), lambda bi, h, i, j: (bi, h, 0, 0)),
            pl.BlockSpec((1, 1, lc, dv), lambda bi, h, i, j: (bi, h, 0, 0)),
        ]
        args += [kc, vc]
    return pl.pallas_call(
        functools.partial(_flash_kernel, has_ctx=has_ctx, n_sub=n_sub),
        grid=(b, hh, sq // tq, lk // tk),
        in_specs=in_specs,
        out_specs=pl.BlockSpec((1, tq, MLA_V), lambda bi, h, i, j: (bi, i, h)),
        out_shape=jax.ShapeDtypeStruct((b, sq, hh * MLA_V), BF16),
        scratch_shapes=[pltpu.VMEM((tq, 1), F32), pltpu.VMEM((tq, dv), F32)],
        compiler_params=_params(("parallel", "parallel", "parallel", "arbitrary")),
        name="mla_flash_ctx" if has_ctx else "mla_flash",
    )(*args)


def _post_ffn_kernel(x_ref, z_ref, mod_ref, wo_ref, gffn_ref, w1_ref, w3_ref, w2_ref, o_ref,
                     x1_sc, h_sc, acc_sc):
    f = pl.program_id(2)
    mod = mod_ref[0]

    @pl.when(f == 0)
    def _():
        x1 = x_ref[0] + mod[2:3] * _dot(z_ref[0], wo_ref[...])
        x1_sc[...] = x1
        h_sc[...] = _norm_mod(x1, gffn_ref[...], mod[3:4], mod[4:5]).astype(BF16)
        acc_sc[...] = jnp.zeros(acc_sc.shape, F32)

    h = h_sc[...]
    a = _dot(h, w1_ref[...])
    g = (a * jax.nn.sigmoid(a)) * _dot(h, w3_ref[...])
    acc_sc[...] += _dot(g.astype(BF16), w2_ref[...])

    @pl.when(f == pl.num_programs(2) - 1)
    def _():
        o_ref[0] = x1_sc[...] + mod[5:6] * acc_sc[...]


def _post_ffn(x, z, mod, wo, gffn, w1, w3, w2, tm, tf):
    b, l, d = x.shape
    dff = w1.shape[1]
    per_batch_mod = mod.shape[0] > 1
    mod_map = (lambda bi, i, f: (bi, 0, 0)) if per_batch_mod else (lambda bi, i, f: (0, 0, 0))
    return pl.pallas_call(
        _post_ffn_kernel,
        grid=(b, l // tm, dff // tf),
        in_specs=[
            pl.BlockSpec((1, tm, d), lambda bi, i, f: (bi, i, 0)),
            pl.BlockSpec((1, tm, d), lambda bi, i, f: (bi, i, 0)),
            pl.BlockSpec((1, 6, d), mod_map),
            pl.BlockSpec((d, d), lambda bi, i, f: (0, 0)),
            pl.BlockSpec((1, d), lambda bi, i, f: (0, 0)),
            pl.BlockSpec((d, tf), lambda bi, i, f: (0, f)),
            pl.BlockSpec((d, tf), lambda bi, i, f: (0, f)),
            pl.BlockSpec((tf, d), lambda bi, i, f: (f, 0)),
        ],
        out_specs=pl.BlockSpec((1, tm, d), lambda bi, i, f: (bi, i, 0)),
        out_shape=jax.ShapeDtypeStruct((b, l, d), F32),
        scratch_shapes=[pltpu.VMEM((tm, d), F32), pltpu.VMEM((tm, d), BF16), pltpu.VMEM((tm, d), F32)],
        compiler_params=_params(("parallel", "parallel", "arbitrary")),
        name="post_ffn",
    )(x, z, mod, wo, gffn, w1, w3, w2)


def _rwkv_feat_kernel(x_ref, xp_ref, xn_ref, mod_ref, gmix_ref, mix_ref, vec_ref, wr_ref, wk_ref, wv_ref,
                      dw1_ref, dw2_ref, ia1_ref, ia2_ref, g1_ref, g2_ref, hsum_ref,
                      r_ref, v_ref, g_ref, kk_ref, kd_ref, lw_ref, b_ref):
    i = pl.program_id(1)
    nt = pl.num_programs(1)
    mod = mod_ref[0]
    gmix = gmix_ref[...]
    tm, d = x_ref.shape[1], x_ref.shape[2]
    h = _norm_mod(x_ref[0], gmix, mod[0:1], mod[1:2])
    hp = _norm_mod(xp_ref[0], gmix, mod[0:1], mod[1:2])[7:8]
    hn = _norm_mod(xn_ref[0], gmix, mod[0:1], mod[1:2])[0:1]
    hp = jnp.where(i > 0, hp, 0.0)
    hn = jnp.where(i < nt - 1, hn, 0.0)
    row = lax.broadcasted_iota(jnp.int32, (tm, 1), 0)
    prev = jnp.where(row == 0, hp, pltpu.roll(h, 1, 0))
    nxt = jnp.where(row == tm - 1, hn, pltpu.roll(h, tm - 1, 0))
    xx = 0.5 * (prev + nxt) - h
    mix = mix_ref[...]
    vec = vec_ref[...]

    def mixed(jj):
        return (h + xx * mix[jj:jj + 1]).astype(BF16)

    r_ref[0] = _dot(mixed(0), wr_ref[...])
    k = _dot(mixed(2), wk_ref[...])
    v_ref[0] = _dot(mixed(3), wv_ref[...])
    g_ref[0] = _dot(jax.nn.sigmoid(_dot(mixed(5), g1_ref[...])).astype(BF16), g2_ref[...])
    dd = _dot(jnp.tanh(_dot(mixed(1), dw1_ref[...])).astype(BF16), dw2_ref[...])
    aa = _dot(_dot(mixed(4), ia1_ref[...]).astype(BF16), ia2_ref[...])
    kkr = k * vec[0:1]
    kk = kkr * lax.rsqrt(_dot((kkr * kkr).astype(BF16), hsum_ref[...]) + 1e-12)
    kk_ref[0] = kk
    for dr in range(2):
        z = -(vec[2 + dr:3 + dr] + dd[:, dr * d:(dr + 1) * d])
        softplus = jnp.maximum(z, 0.0) + jnp.log(1.0 + jnp.exp(-jnp.abs(z)))
        lw_ref[dr, 0] = -jnp.exp(-softplus - 0.5)
        a = jax.nn.sigmoid(vec[4 + dr:5 + dr] + aa[:, dr * d:(dr + 1) * d])
        kd_ref[dr, 0] = k * (1.0 + (a - 1.0) * vec[1:2])
        b_ref[dr, 0] = kk * a


def _rwkv_feat(x, mod, gmix, mix, vec, wts, tm):
    b, l, d = x.shape
    per_batch_mod = mod.shape[0] > 1
    mod_map = (lambda bi, i: (bi, 0, 0)) if per_batch_mod else (lambda bi, i: (0, 0, 0))
    t8 = tm // 8
    last8 = l // 8 - 1
    tok = pl.BlockSpec((1, tm, d), lambda bi, i: (bi, i, 0))
    tok2 = pl.BlockSpec((2, 1, tm, d), lambda bi, i: (0, bi, i, 0))
    one = jax.ShapeDtypeStruct((b, l, d), F32)
    two = jax.ShapeDtypeStruct((2, b, l, d), F32)
    return pl.pallas_call(
        _rwkv_feat_kernel,
        grid=(b, l // tm),
        in_specs=[
            tok,
            pl.BlockSpec((1, 8, d), lambda bi, i: (bi, jnp.maximum(i * t8 - 1, 0), 0)),
            pl.BlockSpec((1, 8, d), lambda bi, i: (bi, jnp.minimum((i + 1) * t8, last8), 0)),
            pl.BlockSpec((1, 6, d), mod_map),
            _full(gmix.shape), _full(mix.shape), _full(vec.shape),
        ] + [_full(w.shape) for w in wts],
        out_specs=[tok, tok, tok, tok, tok2, tok2, tok2],
        out_shape=[one, one, one, one, two, two, two],
        compiler_params=_params(("parallel", "parallel")),
        name="rwkv_feat",
    )(x, x, x, mod, gmix, mix, vec, *wts)


def _wkv_kernel(r_ref, v_ref, kk_ref, kd_ref, lw_ref, b_ref, s0_ref, y_ref, sT_ref, st_sc):
    dr = pl.program_id(0)
    s = pl.program_id(1)
    nb = r_ref.shape[0]
    c = WKV_CHUNK
    n = RW_HEAD
    gw = WKV_GROUP * n
    d = r_ref.shape[2]
    groups = range(d // gw)

    @pl.when(s == 0)
    def _():
        st_sc[...] = s0_ref[0]

    sign = 1 - 2 * dr
    tt = lax.broadcasted_iota(jnp.int32, (c, c), 0)
    ss = lax.broadcasted_iota(jnp.int32, (c, c), 1)
    incl = (tt - ss) * sign >= 0
    t4 = lax.broadcasted_iota(jnp.int32, (c, gw), 0)
    s4 = lax.broadcasted_iota(jnp.int32, (c, gw), 1) % n
    ahead4 = (t4 - s4) * sign
    incl4 = ahead4 >= 0
    strict4 = ahead4 > 0
    eye4 = (t4 == s4).astype(F32)
    lane_head = lax.broadcasted_iota(jnp.int32, (c, 2 * n), 1) // n
    row_blk = lax.broadcasted_iota(jnp.int32, (gw, gw), 0) // n
    col_blk = lax.broadcasted_iota(jnp.int32, (gw, gw), 1) // n
    diag_blk = row_blk == col_blk

    def bdiag(x):
        xb = x.astype(BF16)
        zero = jnp.zeros((c, 2 * n), BF16)
        rows = []
        for hd in range(WKV_GROUP):
            tile = xb[:, (hd // 2) * 2 * n:(hd // 2 + 1) * 2 * n]
            tile = jnp.where(lane_head == hd % 2, tile, zero)
            rows.append(jnp.concatenate([tile, zero] if hd < 2 else [zero, tile], axis=1))
        return jnp.concatenate(rows, axis=0)

    tri = incl.astype(BF16)
    last = jnp.where(dr == 0, c - 1, 0)
    rowsel = lax.broadcasted_iota(jnp.int32, (c, 1), 0) == last
    at, rt, bt, kt, bw, kw, v, w_end = [], [], [], [], [], [], [], []
    for bi in range(nb):
        lw = lw_ref[0, bi]
        lw_hi = lw.astype(BF16)
        lw_lo = (lw - lw_hi.astype(F32)).astype(BF16)
        cum = _dot(tri, lw_hi) + _dot(tri, lw_lo)
        e_pos = jnp.exp(cum)
        e_neg = jnp.exp(-cum)
        we = jnp.sum(jnp.where(rowsel, e_pos, 0.0), axis=0, keepdims=True)
        rt.append(r_ref[bi] * e_pos)
        at.append(-kk_ref[bi] * jnp.exp(cum - lw))
        bt.append(b_ref[0, bi] * e_neg)
        kt.append(kd_ref[0, bi] * e_neg)
        bw.append(bt[bi] * we)
        kw.append(kt[bi] * we)
        v.append(v_ref[bi])
        w_end.append(we)

    units = [(bi, g) for bi in range(nb) for g in groups]
    nu = range(len(units))
    sl = [slice(g * gw, (g + 1) * gw) for _, g in units]
    bix = [bi for bi, _ in units]
    s0 = [st_sc[bi, g] for bi, g in units]
    ar = [jnp.concatenate([at[bix[i]][:, sl[i]], rt[bix[i]][:, sl[i]]], axis=0).astype(BF16) for i in nu]
    v_bd = [bdiag(v[bix[i]][:, sl[i]]) for i in nu]
    pb = [_dot_nt(ar[i], bdiag(bt[bix[i]][:, sl[i]])) for i in nu]
    pk = [_dot_nt(ar[i], bdiag(kt[bix[i]][:, sl[i]])) for i in nu]
    ars = [_dot_nt(ar[i], s0[i].astype(BF16)) for i in nu]
    l_ab = [jnp.where(strict4, pb[i][:c], 0.0) for i in nu]
    lakv = [_dot(jnp.where(strict4, pk[i][:c], 0.0).astype(BF16), v_bd[i]) for i in nu]
    m = [jnp.concatenate([jnp.where(incl4, pb[i][c:], 0.0), jnp.where(incl4, pk[i][c:], 0.0)],
                         axis=1).astype(BF16) for i in nu]
    tinv = [eye4 + l_ab[i] for i in nu]
    lp = [_dot(l_ab[i].astype(BF16), bdiag(l_ab[i])) for i in nu]
    for step in range(5):
        lp_b = [lp[i].astype(BF16) for i in nu]
        upd = [_dot(lp_b[i], bdiag(tinv[i])) for i in nu]
        if step < 4:
            lp = [_dot(lp_b[i], bdiag(lp[i])) for i in nu]
        tinv = [tinv[i] + upd[i] for i in nu]
    x = [ars[i][:c] + lakv[i] for i in nu]
    u = [_dot(tinv[i].astype(BF16), bdiag(x[i])) for i in nu]
    yg = [ars[i][c:] + _dot(m[i], jnp.concatenate([bdiag(u[i]), v_bd[i]], axis=0)) for i in nu]
    uv = [jnp.concatenate([u[i], v[bix[i]][:, sl[i]]], axis=0).astype(BF16) for i in nu]
    bkw = [jnp.concatenate([bw[bix[i]][:, sl[i]], kw[bix[i]][:, sl[i]]], axis=0).astype(BF16) for i in nu]
    s1 = [jnp.where(diag_blk, s0[i] * w_end[bix[i]][:, sl[i]] + _dot_tn(uv[i], bkw[i]), 0.0) for i in nu]
    for i, (bi, g) in enumerate(units):
        y_ref[0, bi, :, sl[i]] = yg[i]
        st_sc[bi, g] = s1[i]

    @pl.when(s == pl.num_programs(1) - 1)
    def _():
        sT_ref[0] = st_sc[...]


def _wkv(r, v, kk, kd, lw, bb, s0):
    b, l, d = r.shape
    c = WKV_CHUNK
    nc = l // c
    gw = WKV_GROUP * RW_HEAD
    ng = d // gw
    assert WKV_GROUP == 4 and RW_HEAD == 64 and d % gw == 0

    def cidx(dr, s):
        return s + dr * (nc - 1 - 2 * s)

    tok = pl.BlockSpec((b, c, d), lambda dr, s: (0, cidx(dr, s), 0))
    tok2 = pl.BlockSpec((1, b, c, d), lambda dr, s: (dr, 0, cidx(dr, s), 0))
    st = pl.BlockSpec((1, b, ng, gw, gw), lambda dr, s: (dr, 0, 0, 0, 0))
    return pl.pallas_call(
        _wkv_kernel,
        grid=(2, nc),
        in_specs=[tok, tok, tok, tok2, tok2, tok2, st],
        out_specs=[tok2, st],
        out_shape=[jax.ShapeDtypeStruct((2, b, l, d), F32),
                   jax.ShapeDtypeStruct((2, b, ng, gw, gw), F32)],
        scratch_shapes=[pltpu.VMEM((b, ng, gw, gw), F32)],
        compiler_params=_params(("parallel", "arbitrary")),
        name="wkv7",
    )(r, v, kk, kd, lw, bb, s0)


def _rwkv_out_kernel(y_ref, r_ref, kd_ref, v_ref, g_ref, vec_ref, hsum_ref, z_ref):
    vec = vec_ref[...]
    hsum = hsum_ref[...]
    inv_n = 1.0 / RW_HEAD
    y = y_ref[0, 0] + y_ref[1, 0]
    mu = _dot(y.astype(BF16), hsum) * inv_n
    yc = y - mu
    var = _dot((yc * yc).astype(BF16), hsum) * inv_n
    yn = yc * lax.rsqrt(var + RW_GN_EPS) * vec[1:2] + vec[2:3]
    rk = r_ref[0] * (kd_ref[0, 0] + kd_ref[1, 0]) * vec[0:1]
    bonus = _dot(rk.astype(BF16), hsum) * v_ref[0]
    z_ref[0] = ((yn + bonus) * g_ref[0]).astype(z_ref.dtype)


def _rwkv_out(y, r, kd, v, g, vec, hsum, tm):
    b, l, d = r.shape
    tok = pl.BlockSpec((1, tm, d), lambda bi, i: (bi, i, 0))
    tok2 = pl.BlockSpec((2, 1, tm, d), lambda bi, i: (0, bi, i, 0))
    return pl.pallas_call(
        _rwkv_out_kernel,
        grid=(b, l // tm),
        in_specs=[tok2, tok, tok2, tok, tok, _full(vec.shape), _full(hsum.shape)],
        out_specs=tok,
        out_shape=jax.ShapeDtypeStruct((b, l, d), BF16),
        compiler_params=_params(("parallel", "parallel")),
        name="rwkv_out",
    )(y, r, kd, v, g, vec, hsum)


def _rope_table(n_tokens):
    rows = n_tokens // GRID_W
    row = jnp.broadcast_to(jnp.arange(rows, dtype=F32)[:, None], (rows, GRID_W)).reshape(-1)
    col = jnp.broadcast_to(jnp.arange(GRID_W, dtype=F32)[None, :], (rows, GRID_W)).reshape(-1)
    inv = ROPE_BASE ** (-jnp.arange(ROPE_FREQ, dtype=F32) / ROPE_FREQ)
    ar, ac = row[:, None] * inv, col[:, None] * inv
    cr, sr, cc, sc = jnp.cos(ar), jnp.sin(ar), jnp.cos(ac), jnp.sin(ac)
    return jnp.concatenate([cr, cr, cc, cc, -sr, sr, -sc, sc], axis=1)


def _pad8(x):
    return jnp.pad(x, ((0, 8 - x.shape[0]), (0, 0)))


def kernel(x, c, ctx, c_ctx, ada_w, ada_b, norm_mix, norm_ffn, ffn_w1, ffn_w3, ffn_w2, mla_w_dqkv, mla_g_q_lora, mla_g_kv_lora, mla_w_uq, mla_w_ukv, mla_g_qn, mla_g_kn, mla_w_o, rw_mix, rw_w_r, rw_w_k, rw_w_v, rw_w_o, rw_k_k, rw_k_a, rw_r_k, rw_decay_w0, rw_decay_w1, rw_decay_w2, rw_iclr_a0, rw_iclr_a1, rw_iclr_a2, rw_gate_g1, rw_gate_g2, rw_gn_w, rw_gn_b):
    bsz, seq, d = x.shape
    n_ctx = ctx.shape[1]
    hh = MLA_HEADS

    cc = _pad8(jnp.concatenate([c, c_ctx[None, :]], axis=0))
    mod = _adaln_mod(cc, ada_w, ada_b)
    mod_l = [mod[i, :bsz].reshape(bsz, 6, d) for i in range(2)]
    mod_c = [mod[i, bsz:bsz + 1].reshape(1, 6, d) for i in range(2)]

    swap = np.arange(MLA_ROPE) ^ ROPE_FREQ
    wdq = mla_w_dqkv[0]
    rope0 = MLA_Q_LORA + MLA_KV_LORA
    wd = jnp.concatenate([wdq, wdq[:, rope0 + swap]], axis=1).astype(BF16)
    wuq = mla_w_uq[0].reshape(MLA_Q_LORA, hh, MLA_QK)
    wuq = jnp.concatenate([wuq, wuq[:, :, MLA_NOPE + swap]], axis=2).reshape(MLA_Q_LORA, hh * MLA_QK_PAD).astype(BF16)
    wukv = mla_w_ukv[0].astype(BF16)
    gqn, gkn = mla_g_qn[0], mla_g_kn[0]
    gq = jnp.concatenate([gqn, gqn[MLA_NOPE + swap]])[None, :]
    gk = jnp.concatenate([gkn, gkn[MLA_NOPE + swap]])[None, :]
    proj_w = (wd, mla_g_q_lora[0][None, :], mla_g_kv_lora[0][None, :], wuq, wukv, gq, gk)
    tab_l = _rope_table(seq)
    tab_c = jnp.concatenate([jnp.ones((n_ctx, MLA_ROPE), F32), jnp.zeros((n_ctx, MLA_ROPE), F32)], axis=1)
    gmix0 = norm_mix[0][None, :]
    tm_l = min(512, seq)
    tm_c = min(256, n_ctx)
    q_l, k_l, v_l = _mla_proj(x, mod_l[0], gmix0, proj_w, tab_l, tm_l)
    q_c, k_c, v_c = _mla_proj(ctx, mod_c[0], gmix0, proj_w, tab_c, tm_c)
    tk_l = min(8192, seq)
    o_l = _flash(q_l, k_l, v_l, k_c, v_c, min(1024, seq), tk_l, max(tk_l // FLASH_KEYS, 1))
    o_c = _flash(q_c, k_c, v_c, None, None, tm_c, n_ctx, 1)

    w1 = ffn_w1.astype(BF16)
    w3 = ffn_w3.astype(BF16)
    w2 = ffn_w2.astype(BF16)
    tf = 256
    wo0 = mla_w_o[0].astype(BF16)
    x1 = _post_ffn(x, o_l, mod_l[0], wo0, norm_ffn[0][None, :], w1[0], w3[0], w2[0], tm_l, tf)
    ctx1 = _post_ffn(ctx, o_c, mod_c[0], wo0, norm_ffn[0][None, :], w1[0], w3[0], w2[0], tm_c, tf)

    zpad = RW_GATE_PAD - RW_GATE_LORA
    dw2 = rw_decay_w2[0]
    ia2 = rw_iclr_a2[0]
    zero = jnp.zeros_like(dw2[0])
    feat_w = (
        rw_w_r[0].astype(BF16), rw_w_k[0].astype(BF16), rw_w_v[0].astype(BF16),
        jnp.concatenate([rw_decay_w1[0, 0], rw_decay_w1[0, 1]], axis=1).astype(BF16),
        jnp.concatenate([jnp.concatenate([dw2[0], zero], 1), jnp.concatenate([zero, dw2[1]], 1)], 0).astype(BF16),
        jnp.concatenate([rw_iclr_a1[0, 0], rw_iclr_a1[0, 1]], axis=1).astype(BF16),
        jnp.concatenate([jnp.concatenate([ia2[0], zero], 1), jnp.concatenate([zero, ia2[1]], 1)], 0).astype(BF16),
        jnp.pad(rw_gate_g1[0], ((0, 0), (0, zpad))).astype(BF16),
        jnp.pad(rw_gate_g2[0], ((0, zpad), (0, 0))).astype(BF16),
    )
    head_of = np.arange(d) // RW_HEAD
    hsum = jnp.asarray(head_of[:, None] == head_of[None, :], BF16)
    vec = _pad8(jnp.stack([rw_k_k[0], rw_k_a[0], rw_decay_w0[0, 0], rw_decay_w0[0, 1],
                           rw_iclr_a0[0, 0], rw_iclr_a0[0, 1]]))
    gmix1 = norm_mix[1][None, :]
    tmf = 256
    r_l, v_l2, g_l, kk_l, kd_l, lw_l, b_l = _rwkv_feat(x1, mod_l[1], gmix1, rw_mix[0], vec, feat_w + (hsum,), min(tmf, seq))
    r_c, v_c2, _, kk_c, kd_c, lw_c, b_c = _rwkv_feat(ctx1, mod_c[1], gmix1, rw_mix[0], vec, feat_w + (hsum,), min(tmf, n_ctx))
    gw = WKV_GROUP * RW_HEAD
    s_zero = jnp.zeros((2, bsz, d // gw, gw, gw), F32)
    _, s_ctx = _wkv(r_c, v_c2, kk_c, kd_c, lw_c, b_c, s_zero)
    y, _ = _wkv(r_l, v_l2, kk_l, kd_l, lw_l, b_l, s_ctx)
    ovec = _pad8(jnp.stack([rw_r_k[0].reshape(-1), rw_gn_w[0], rw_gn_b[0]]))
    z = _rwkv_out(y, r_l, kd_l, v_l2, g_l, ovec, hsum, min(256, seq))
    return _post_ffn(x1, z, mod_l[1], rw_w_o[0].astype(BF16), norm_ffn[1][None, :], w1[1], w3[1], w2[1], tm_l, tf)
```

```python
import functools

import jax
import jax.numpy as jnp
import numpy as np
from jax import lax
from jax.experimental import pallas as pl
from jax.experimental.pallas import tpu as pltpu

F32 = jnp.float32
BF16 = jnp.bfloat16
HIGHEST = lax.Precision.HIGHEST

EPS = 1e-6
GRID_W = 64
MLA_HEADS = 8
MLA_Q_LORA = 384
MLA_KV_LORA = 256
MLA_NOPE = 128
MLA_ROPE = 64
MLA_V = 128
MLA_QK = MLA_NOPE + MLA_ROPE
MLA_QK_PAD = 256
MLA_V_PAD = 256
Q_SCALE = float(np.log2(np.e)) * MLA_QK ** -0.5
ROPE_FREQ = MLA_ROPE // 4
ROPE_BASE = 10000.0
RW_HEAD = 64
RW_GATE_LORA = 160
RW_GATE_PAD = 256
RW_GN_EPS = 64e-5
DECAY_SCALE = float(np.exp(-0.5))
WKV_CHUNK = 64
WKV_GROUP = 4
FLASH_ROWS = 256
FLASH_KEYS = 4096

VMEM_LIMIT_BYTES = 56 * 1024 * 1024


def _params(sem):
    return pltpu.CompilerParams(dimension_semantics=sem, vmem_limit_bytes=VMEM_LIMIT_BYTES)


def _dot(a, b):
    return jnp.dot(a, b, preferred_element_type=F32)


def _dot_nt(a, b):
    return lax.dot_general(a, b, (((1,), (1,)), ((), ())), preferred_element_type=F32)


def _dot_tn(a, b):
    return lax.dot_general(a, b, (((0,), (0,)), ((), ())), preferred_element_type=F32)


def _norm_mod(x, g, shift, scale):
    ms = jnp.mean(x * x, axis=-1, keepdims=True)
    return (x * lax.rsqrt(ms + EPS) * g) * (1.0 + scale) + shift


def _rms(x, g):
    ms = jnp.mean(x * x, axis=-1, keepdims=True)
    return x * lax.rsqrt(ms + EPS) * g


def _full(shape):
    n = len(shape)
    return pl.BlockSpec(shape, lambda *_: (0,) * n)


def _mod_kernel(cc_ref, w_ref, b_ref, o_ref):
    cc = cc_ref[...]
    s = cc * jax.nn.sigmoid(cc)
    o_ref[0] = jnp.dot(s, w_ref[0], precision=HIGHEST, preferred_element_type=F32) + b_ref[0]


def _adaln_mod(cc, ada_w, ada_b):
    depth, d, n = ada_w.shape
    tn = 1536
    rows = cc.shape[0]
    return pl.pallas_call(
        _mod_kernel,
        grid=(depth, n // tn),
        in_specs=[
            pl.BlockSpec((rows, d), lambda i, j: (0, 0)),
            pl.BlockSpec((1, d, tn), lambda i, j: (i, 0, j)),
            pl.BlockSpec((1, 1, tn), lambda i, j: (i, 0, j)),
        ],
        out_specs=pl.BlockSpec((1, rows, tn), lambda i, j: (i, 0, j)),
        out_shape=jax.ShapeDtypeStruct((depth, rows, n), F32),
        compiler_params=_params(("parallel", "parallel")),
        name="adaln_mod",
    )(cc, ada_w, ada_b.reshape(depth, 1, n))


def _mla_proj_kernel(x_ref, mod_ref, gmix_ref, wd_ref, glq_ref, glkv_ref, wuq_ref, wukv_ref,
                     gq_ref, gk_ref, tab_ref, q_ref, k_ref, v_ref):
    mod = mod_ref[0]
    h = _norm_mod(x_ref[0], gmix_ref[...], mod[0:1], mod[1:2])
    down = _dot(h.astype(BF16), wd_ref[...])
    cq = _rms(down[:, :MLA_Q_LORA], glq_ref[...]).astype(BF16)
    ckv = _rms(down[:, MLA_Q_LORA:MLA_Q_LORA + MLA_KV_LORA], glkv_ref[...]).astype(BF16)
    kr = down[:, MLA_Q_LORA + MLA_KV_LORA:]
    tab = tab_ref[...]
    lo = lax.broadcasted_iota(jnp.int32, (1, 128), 1) < MLA_ROPE
    gq = gq_ref[...]
    gk = gk_ref[...]

    def rope(t):
        p = t * tab
        return jnp.where(lo, p + pltpu.roll(p, MLA_ROPE, 1), 0.0)

    kr_ss = jnp.sum(jnp.where(lo, kr * kr, 0.0), axis=-1, keepdims=True)
    k_rot = rope(kr * gk[:, 128:])
    ones_col = (lax.broadcasted_iota(jnp.int32, (x_ref.shape[1], 128), 1) == 0).astype(BF16)
    for hd in range(MLA_HEADS):
        c0 = hd * MLA_QK_PAD
        qh = _dot(cq, wuq_ref[:, c0:c0 + MLA_QK_PAD])
        qn, qr = qh[:, :128], qh[:, 128:]
        ssq = jnp.sum(qn * qn, axis=-1, keepdims=True) + jnp.sum(jnp.where(lo, qr * qr, 0.0), axis=-1, keepdims=True)
        rq = lax.rsqrt(ssq * (1.0 / MLA_QK) + EPS) * Q_SCALE
        q_ref[0, hd, :, :128] = (qn * gq[:, :128] * rq).astype(BF16)
        q_ref[0, hd, :, 128:] = (rope(qr * gq[:, 128:]) * rq).astype(BF16)
        kvh = _dot(ckv, wukv_ref[:, c0:c0 + MLA_QK_PAD])
        kn = kvh[:, :128]
        rk = lax.rsqrt((jnp.sum(kn * kn, axis=-1, keepdims=True) + kr_ss) * (1.0 / MLA_QK) + EPS)
        k_ref[0, hd, :, :128] = (kn * gk[:, :128] * rk).astype(BF16)
        k_ref[0, hd, :, 128:] = (k_rot * rk).astype(BF16)
        v_ref[0, hd, :, :MLA_V] = kvh[:, 128:].astype(BF16)
        v_ref[0, hd, :, MLA_V:] = ones_col


def _mla_proj(x, mod, gmix, wts, tab, tm):
    b, l, d = x.shape
    hh = MLA_HEADS
    per_batch_mod = mod.shape[0] > 1
    mod_map = (lambda bi, i: (bi, 0, 0)) if per_batch_mod else (lambda bi, i: (0, 0, 0))
    wd, glq, glkv, wuq, wukv, gq, gk = wts
    return pl.pallas_call(
        _mla_proj_kernel,
        grid=(b, l // tm),
        in_specs=[
            pl.BlockSpec((1, tm, d), lambda bi, i: (bi, i, 0)),
            pl.BlockSpec((1, 6, d), mod_map),
            _full(gmix.shape), _full(wd.shape), _full(glq.shape), _full(glkv.shape),
            _full(wuq.shape), _full(wukv.shape), _full(gq.shape), _full(gk.shape),
            pl.BlockSpec((tm, 128), lambda bi, i: (i, 0)),
        ],
        out_specs=[
            pl.BlockSpec((1, hh, tm, MLA_QK_PAD), lambda bi, i: (bi, 0, i, 0)),
            pl.BlockSpec((1, hh, tm, MLA_QK_PAD), lambda bi, i: (bi, 0, i, 0)),
            pl.BlockSpec((1, hh, tm, MLA_V_PAD), lambda bi, i: (bi, 0, i, 0)),
        ],
        out_shape=[
            jax.ShapeDtypeStruct((b, hh, l, MLA_QK_PAD), BF16),
            jax.ShapeDtypeStruct((b, hh, l, MLA_QK_PAD), BF16),
            jax.ShapeDtypeStruct((b, hh, l, MLA_V_PAD), BF16),
        ],
        compiler_params=_params(("parallel", "parallel")),
        name="mla_proj",
    )(x, mod, gmix, wd, glq, glkv, wuq, wukv, gq, gk, tab)


def _flash_kernel(*refs, has_ctx, n_sub):
    if has_ctx:
        q_ref, k_ref, v_ref, kc_ref, vc_ref, o_ref, m_sc, acc_sc = refs
    else:
        q_ref, k_ref, v_ref, o_ref, m_sc, acc_sc = refs
    j = pl.program_id(3)
    tr = min(FLASH_ROWS, q_ref.shape[2])
    n_row = q_ref.shape[2] // tr

    def attend(kv_ref, vv_ref, n_chunk):
        ts = kv_ref.shape[2] // n_chunk
        pairs = [(c, r) for c in range(n_chunk) for r in range(n_row)]
        m_run = [m_sc[r * tr:(r + 1) * tr] for r in range(n_row)]
        s_of, p_of = {}, {}
        for t in range(len(pairs) + 2):
            if t < len(pairs):
                c, r = pairs[t]
                s_of[t] = _dot_nt(q_ref[0, 0, r * tr:(r + 1) * tr], kv_ref[0, 0, c * ts:(c + 1) * ts])
            if 0 <= t - 1 < len(pairs):
                c, r = pairs[t - 1]
                s = s_of.pop(t - 1)
                m_new = jnp.maximum(m_run[r], jnp.max(s, axis=-1, keepdims=True))
                p_of[t - 1] = (jnp.exp2(s - m_new).astype(BF16), jnp.exp2(m_run[r] - m_new))
                m_run[r] = m_new
            if 0 <= t - 2 < len(pairs):
                c, r = pairs[t - 2]
                p, alpha = p_of.pop(t - 2)
                rows = slice(r * tr, (r + 1) * tr)
                acc_sc[rows] = alpha * acc_sc[rows] + _dot(p, vv_ref[0, 0, c * ts:(c + 1) * ts])
        for r in range(n_row):
            m_sc[r * tr:(r + 1) * tr] = m_run[r]

    @pl.when(j == 0)
    def _():
        m_sc[...] = jnp.full(m_sc.shape, -1e30, F32)
        acc_sc[...] = jnp.zeros(acc_sc.shape, F32)
        if has_ctx:
            attend(kc_ref, vc_ref, 1)

    attend(k_ref, v_ref, n_sub)

    @pl.when(j == pl.num_programs(3) - 1)
    def _():
        acc = acc_sc[...]
        o_ref[0] = (acc[:, :MLA_V] / acc[:, MLA_V:MLA_V + 1]).astype(o_ref.dtype)


def _flash(q, k, v, kc, vc, tq, tk, n_sub):
    b, hh, sq, dq = q.shape
    lk = k.shape[2]
    dv = v.shape[3]
    has_ctx = kc is not None
    in_specs = [
        pl.BlockSpec((1, 1, tq, dq), lambda bi, h, i, j: (bi, h, i, 0)),
        pl.BlockSpec((1, 1, tk, dq), lambda bi, h, i, j: (bi, h, j, 0)),
        pl.BlockSpec((1, 1, tk, dv), lambda bi, h, i, j: (bi, h, j, 0)),
    ]
    args = [q, k, v]
    if has_ctx:
        lc = kc.shape[2]
        in_specs += [
            pl.BlockSpec((1, 1, lc, dq), lambda bi, h, i, j: (bi, h, 0, 0)),
            pl.BlockSpec((1, 1, lc, dv), lambda bi, h, i, j: (bi, h, 0, 0)),
        ]
        args += [kc, vc]
    return pl.pallas_call(
        functools.partial(_flash_kernel, has_ctx=has_ctx, n_sub=n_sub),
        grid=(b, hh, sq // tq, lk // tk),
        in_specs=in_specs,
        out_specs=pl.BlockSpec((1, tq, MLA_V), lambda bi, h, i, j: (bi, i, h)),
        out_shape=jax.ShapeDtypeStruct((b, sq, hh * MLA_V), BF16),
        scratch_shapes=[pltpu.VMEM((tq, 1), F32), pltpu.VMEM((tq, dv), F32)],
        compiler_params=_params(("parallel", "parallel", "parallel", "arbitrary")),
        name="mla_flash_ctx" if has_ctx else "mla_flash",
    )(*args)


def _post_ffn_kernel(x_ref, z_ref, mod_ref, wo_ref, gffn_ref, w1_ref, w3_ref, w2_ref, o_ref,
                     h_sc, acc_sc):
    f = pl.program_id(2)
    mod = mod_ref[0]

    @pl.when(f == 0)
    def _():
        x1 = x_ref[0] + mod[2:3] * _dot(z_ref[0], wo_ref[...])
        o_ref[0] = x1
        h_sc[...] = _norm_mod(x1, gffn_ref[...], mod[3:4], mod[4:5]).astype(BF16)
        acc_sc[...] = jnp.zeros(acc_sc.shape, F32)

    h = h_sc[...]
    a = _dot(h, w1_ref[...])
    g = (a * jax.nn.sigmoid(a)) * _dot(h, w3_ref[...])
    acc_sc[...] += _dot(g.astype(BF16), w2_ref[...])

    @pl.when(f == pl.num_programs(2) - 1)
    def _():
        o_ref[0] = o_ref[0] + mod[5:6] * acc_sc[...]


def _post_ffn(x, z, mod, wo, gffn, w1, w3, w2, tm, tf):
    b, l, d = x.shape
    dff = w1.shape[1]
    per_batch_mod = mod.shape[0] > 1
    mod_map = (lambda bi, i, f: (bi, 0, 0)) if per_batch_mod else (lambda bi, i, f: (0, 0, 0))
    return pl.pallas_call(
        _post_ffn_kernel,
        grid=(b, l // tm, dff // tf),
        in_specs=[
            pl.BlockSpec((1, tm, d), lambda bi, i, f: (bi, i, 0)),
            pl.BlockSpec((1, tm, d), lambda bi, i, f: (bi, i, 0)),
            pl.BlockSpec((1, 6, d), mod_map),
            pl.BlockSpec((d, d), lambda bi, i, f: (0, 0)),
            pl.BlockSpec((1, d), lambda bi, i, f: (0, 0)),
            pl.BlockSpec((d, tf), lambda bi, i, f: (0, f)),
            pl.BlockSpec((d, tf), lambda bi, i, f: (0, f)),
            pl.BlockSpec((tf, d), lambda bi, i, f: (f, 0)),
        ],
        out_specs=pl.BlockSpec((1, tm, d), lambda bi, i, f: (bi, i, 0)),
        out_shape=jax.ShapeDtypeStruct((b, l, d), F32),
        scratch_shapes=[pltpu.VMEM((tm, d), BF16), pltpu.VMEM((tm, d), F32)],
        compiler_params=_params(("parallel", "parallel", "arbitrary")),
        name="post_ffn",
    )(x, z, mod, wo, gffn, w1, w3, w2)


def _rwkv_feat_kernel(x_ref, xp_ref, xn_ref, mod_ref, gmix_ref, mix_ref, vec_ref, wr_ref, wk_ref, wv_ref,
                      dw1_ref, dw2_ref, ia1_ref, ia2_ref, g1_ref, g2_ref, hsum_ref,
                      r_ref, v_ref, g_ref, kk_ref, kd_ref, lw_ref, b_ref):
    i = pl.program_id(1)
    nt = pl.num_programs(1)
    mod = mod_ref[0]
    gmix = gmix_ref[...]
    tm, d = x_ref.shape[1], x_ref.shape[2]
    h = _norm_mod(x_ref[0], gmix, mod[0:1], mod[1:2])
    hp = _norm_mod(xp_ref[0], gmix, mod[0:1], mod[1:2])[7:8]
    hn = _norm_mod(xn_ref[0], gmix, mod[0:1], mod[1:2])[0:1]
    hp = jnp.where(i > 0, hp, 0.0)
    hn = jnp.where(i < nt - 1, hn, 0.0)
    row = lax.broadcasted_iota(jnp.int32, (tm, 1), 0)
    prev = jnp.where(row == 0, hp, pltpu.roll(h, 1, 0))
    nxt = jnp.where(row == tm - 1, hn, pltpu.roll(h, tm - 1, 0))
    xx = 0.5 * (prev + nxt) - h
    mix = mix_ref[...]
    vec = vec_ref[...]

    def mixed(jj):
        return (h + xx * mix[jj:jj + 1]).astype(BF16)

    r_ref[0] = _dot(mixed(0), wr_ref[...]).astype(r_ref.dtype)
    k = _dot(mixed(2), wk_ref[...])
    v_ref[0] = _dot(mixed(3), wv_ref[...]).astype(v_ref.dtype)
    g_ref[0] = _dot(jax.nn.sigmoid(_dot(mixed(5), g1_ref[...])).astype(BF16), g2_ref[...]).astype(g_ref.dtype)
    dd = _dot(jnp.tanh(_dot(mixed(1), dw1_ref[...])).astype(BF16), dw2_ref[...])
    aa = _dot(_dot(mixed(4), ia1_ref[...]).astype(BF16), ia2_ref[...])
    kkr = k * vec[0:1]
    kk = kkr * lax.rsqrt(_dot((kkr * kkr).astype(BF16), hsum_ref[...]) + 1e-12)
    kk_ref[0] = kk.astype(kk_ref.dtype)
    for dr in range(2):
        lw_ref[dr, 0] = -DECAY_SCALE * jax.nn.sigmoid(vec[2 + dr:3 + dr] + dd[:, dr * d:(dr + 1) * d])
        a = jax.nn.sigmoid(vec[4 + dr:5 + dr] + aa[:, dr * d:(dr + 1) * d])
        kd_ref[dr, 0] = (k * (1.0 + (a - 1.0) * vec[1:2])).astype(kd_ref.dtype)
        b_ref[dr, 0] = (kk * a).astype(b_ref.dtype)


def _rwkv_feat(x, mod, gmix, mix, vec, wts, tm):
    b, l, d = x.shape
    per_batch_mod = mod.shape[0] > 1
    mod_map = (lambda bi, i: (bi, 0, 0)) if per_batch_mod else (lambda bi, i: (0, 0, 0))
    t8 = tm // 8
    last8 = l // 8 - 1
    tok = pl.BlockSpec((1, tm, d), lambda bi, i: (bi, i, 0))
    tok2 = pl.BlockSpec((2, 1, tm, d), lambda bi, i: (0, bi, i, 0))
    one = jax.ShapeDtypeStruct((b, l, d), BF16)
    two = jax.ShapeDtypeStruct((2, b, l, d), BF16)
    two_f32 = jax.ShapeDtypeStruct((2, b, l, d), F32)
    return pl.pallas_call(
        _rwkv_feat_kernel,
        grid=(b, l // tm),
        in_specs=[
            tok,
            pl.BlockSpec((1, 8, d), lambda bi, i: (bi, jnp.maximum(i * t8 - 1, 0), 0)),
            pl.BlockSpec((1, 8, d), lambda bi, i: (bi, jnp.minimum((i + 1) * t8, last8), 0)),
            pl.BlockSpec((1, 6, d), mod_map),
            _full(gmix.shape), _full(mix.shape), _full(vec.shape),
        ] + [_full(w.shape) for w in wts],
        out_specs=[tok, tok, tok, tok, tok2, tok2, tok2],
        out_shape=[one, one, one, one, two, two_f32, two],
        compiler_params=_params(("parallel", "parallel")),
        name="rwkv_feat",
    )(x, x, x, mod, gmix, mix, vec, *wts)


def _wkv_kernel(r_ref, v_ref, kk_ref, kd_ref, lw_ref, b_ref, s0_ref, y_ref, sT_ref, st_sc):
    dr = pl.program_id(0)
    s = pl.program_id(1)
    nb = r_ref.shape[0]
    c = WKV_CHUNK
    n = RW_HEAD
    gw = WKV_GROUP * n
    d = r_ref.shape[2]
    groups = range(d // gw)

    @pl.when(s == 0)
    def _():
        st_sc[...] = s0_ref[0]

    sign = 1 - 2 * dr
    tt = lax.broadcasted_iota(jnp.int32, (c, c), 0)
    ss = lax.broadcasted_iota(jnp.int32, (c, c), 1)
    incl = (tt - ss) * sign >= 0
    t4 = lax.broadcasted_iota(jnp.int32, (c, gw), 0)
    s4 = lax.broadcasted_iota(jnp.int32, (c, gw), 1) % n
    ahead4 = (t4 - s4) * sign
    incl4 = ahead4 >= 0
    strict4 = ahead4 > 0
    eye4 = (t4 == s4).astype(F32)
    lane_head = lax.broadcasted_iota(jnp.int32, (c, 2 * n), 1) // n
    row_blk = lax.broadcasted_iota(jnp.int32, (gw, gw), 0) // n
    col_blk = lax.broadcasted_iota(jnp.int32, (gw, gw), 1) // n
    diag_blk = row_blk == col_blk

    def bdiag(x):
        xb = x.astype(BF16)
        zero = jnp.zeros((c, 2 * n), BF16)
        rows = []
        for hd in range(WKV_GROUP):
            tile = xb[:, (hd // 2) * 2 * n:(hd // 2 + 1) * 2 * n]
            tile = jnp.where(lane_head == hd % 2, tile, zero)
            rows.append(jnp.concatenate([tile, zero] if hd < 2 else [zero, tile], axis=1))
        return jnp.concatenate(rows, axis=0)

    tri = incl.astype(BF16)
    last = jnp.where(dr == 0, c - 1, 0)
    rowsel = lax.broadcasted_iota(jnp.int32, (c, 1), 0) == last
    at, rt, bt, kt, bw, kw, v, w_end = [], [], [], [], [], [], [], []
    for bi in range(nb):
        lw = lw_ref[0, bi]
        lw_hi = lw.astype(BF16)
        lw_lo = (lw - lw_hi.astype(F32)).astype(BF16)
        cum = _dot(tri, lw_hi) + _dot(tri, lw_lo)
        e_pos = jnp.exp(cum)
        e_neg = jnp.exp(-cum)
        we = jnp.sum(jnp.where(rowsel, e_pos, 0.0), axis=0, keepdims=True)
        rt.append(r_ref[bi].astype(F32) * e_pos)
        at.append(-kk_ref[bi].astype(F32) * jnp.exp(cum - lw))
        bt.append(b_ref[0, bi].astype(F32) * e_neg)
        kt.append(kd_ref[0, bi].astype(F32) * e_neg)
        bw.append(bt[bi] * we)
        kw.append(kt[bi] * we)
        v.append(v_ref[bi].astype(F32))
        w_end.append(we)

    units = [(bi, g) for bi in range(nb) for g in groups]
    nu = range(len(units))
    sl = [slice(g * gw, (g + 1) * gw) for _, g in units]
    bix = [bi for bi, _ in units]
    s0 = [st_sc[bi, g] for bi, g in units]
    ar = [jnp.concatenate([at[bix[i]][:, sl[i]], rt[bix[i]][:, sl[i]]], axis=0).astype(BF16) for i in nu]
    v_bd = [bdiag(v[bix[i]][:, sl[i]]) for i in nu]
    pb = [_dot_nt(ar[i], bdiag(bt[bix[i]][:, sl[i]])) for i in nu]
    pk = [_dot_nt(ar[i], bdiag(kt[bix[i]][:, sl[i]])) for i in nu]
    ars = [_dot_nt(ar[i], s0[i].astype(BF16)) for i in nu]
    l_ab = [jnp.where(strict4, pb[i][:c], 0.0) for i in nu]
    lakv = [_dot(jnp.where(strict4, pk[i][:c], 0.0).astype(BF16), v_bd[i]) for i in nu]
    m = [jnp.concatenate([jnp.where(incl4, pb[i][c:], 0.0), jnp.where(incl4, pk[i][c:], 0.0)],
                         axis=1).astype(BF16) for i in nu]
    tinv = [eye4 + l_ab[i] for i in nu]
    lp = [_dot(l_ab[i].astype(BF16), bdiag(l_ab[i])) for i in nu]
    for step in range(5):
        lp_b = [lp[i].astype(BF16) for i in nu]
        upd = [_dot(lp_b[i], bdiag(tinv[i])) for i in nu]
        if step < 4:
            lp = [_dot(lp_b[i], bdiag(lp[i])) for i in nu]
        tinv = [tinv[i] + upd[i] for i in nu]
    x = [ars[i][:c] + lakv[i] for i in nu]
    u = [_dot(tinv[i].astype(BF16), bdiag(x[i])) for i in nu]
    yg = [ars[i][c:] + _dot(m[i], jnp.concatenate([bdiag(u[i]), v_bd[i]], axis=0)) for i in nu]
    uv = [jnp.concatenate([u[i], v[bix[i]][:, sl[i]]], axis=0).astype(BF16) for i in nu]
    bkw = [jnp.concatenate([bw[bix[i]][:, sl[i]], kw[bix[i]][:, sl[i]]], axis=0).astype(BF16) for i in nu]
    s1 = [jnp.where(diag_blk, s0[i] * w_end[bix[i]][:, sl[i]] + _dot_tn(uv[i], bkw[i]), 0.0) for i in nu]
    for i, (bi, g) in enumerate(units):
        y_ref[0, bi, :, sl[i]] = yg[i]
        st_sc[bi, g] = s1[i]

    @pl.when(s == pl.num_programs(1) - 1)
    def _():
        sT_ref[0] = st_sc[...]


def _wkv(r, v, kk, kd, lw, bb, s0):
    b, l, d = r.shape
    c = WKV_CHUNK
    nc = l // c
    gw = WKV_GROUP * RW_HEAD
    ng = d // gw
    assert WKV_GROUP == 4 and RW_HEAD == 64 and d % gw == 0

    def cidx(dr, s):
        return s + dr * (nc - 1 - 2 * s)

    tok = pl.BlockSpec((b, c, d), lambda dr, s: (0, cidx(dr, s), 0))
    tok2 = pl.BlockSpec((1, b, c, d), lambda dr, s: (dr, 0, cidx(dr, s), 0))
    st = pl.BlockSpec((1, b, ng, gw, gw), lambda dr, s: (dr, 0, 0, 0, 0))
    return pl.pallas_call(
        _wkv_kernel,
        grid=(2, nc),
        in_specs=[tok, tok, tok, tok2, tok2, tok2, st],
        out_specs=[tok2, st],
        out_shape=[jax.ShapeDtypeStruct((2, b, l, d), F32),
                   jax.ShapeDtypeStruct((2, b, ng, gw, gw), F32)],
        scratch_shapes=[pltpu.VMEM((b, ng, gw, gw), F32)],
        compiler_params=_params(("parallel", "arbitrary")),
        name="wkv7",
    )(r, v, kk, kd, lw, bb, s0)


def _rwkv_out_kernel(y_ref, r_ref, kd_ref, v_ref, g_ref, vec_ref, hred_ref, hexp_ref, z_ref):
    vec = vec_ref[...]
    inv_n = 1.0 / RW_HEAD

    def head_sum(t):
        return _dot(_dot(t.astype(BF16), hred_ref[...]).astype(BF16), hexp_ref[...])

    y = y_ref[0, 0] + y_ref[1, 0]
    mu = head_sum(y) * inv_n
    yc = y - mu
    var = head_sum(yc * yc) * inv_n
    yn = yc * lax.rsqrt(var + RW_GN_EPS) * vec[1:2] + vec[2:3]
    rk = r_ref[0].astype(F32) * (kd_ref[0, 0].astype(F32) + kd_ref[1, 0].astype(F32)) * vec[0:1]
    bonus = head_sum(rk) * v_ref[0].astype(F32)
    z_ref[0] = ((yn + bonus) * g_ref[0].astype(F32)).astype(z_ref.dtype)


def _rwkv_out(y, r, kd, v, g, vec, hred, hexp, tm):
    b, l, d = r.shape
    tok = pl.BlockSpec((1, tm, d), lambda bi, i: (bi, i, 0))
    tok2 = pl.BlockSpec((2, 1, tm, d), lambda bi, i: (0, bi, i, 0))
    return pl.pallas_call(
        _rwkv_out_kernel,
        grid=(b, l // tm),
        in_specs=[tok2, tok, tok2, tok, tok, _full(vec.shape), _full(hred.shape), _full(hexp.shape)],
        out_specs=tok,
        out_shape=jax.ShapeDtypeStruct((b, l, d), BF16),
        compiler_params=_params(("parallel", "parallel")),
        name="rwkv_out",
    )(y, r, kd, v, g, vec, hred, hexp)


def _rope_table(n_tokens):
    rows = n_tokens // GRID_W
    row = jnp.broadcast_to(jnp.arange(rows, dtype=F32)[:, None], (rows, GRID_W)).reshape(-1)
    col = jnp.broadcast_to(jnp.arange(GRID_W, dtype=F32)[None, :], (rows, GRID_W)).reshape(-1)
    inv = ROPE_BASE ** (-jnp.arange(ROPE_FREQ, dtype=F32) / ROPE_FREQ)
    ar, ac = row[:, None] * inv, col[:, None] * inv
    cr, sr, cc, sc = jnp.cos(ar), jnp.sin(ar), jnp.cos(ac), jnp.sin(ac)
    return jnp.concatenate([cr, cr, cc, cc, -sr, sr, -sc, sc], axis=1)


def _pad8(x):
    return jnp.pad(x, ((0, 8 - x.shape[0]), (0, 0)))


def kernel(x, c, ctx, c_ctx, ada_w, ada_b, norm_mix, norm_ffn, ffn_w1, ffn_w3, ffn_w2, mla_w_dqkv, mla_g_q_lora, mla_g_kv_lora, mla_w_uq, mla_w_ukv, mla_g_qn, mla_g_kn, mla_w_o, rw_mix, rw_w_r, rw_w_k, rw_w_v, rw_w_o, rw_k_k, rw_k_a, rw_r_k, rw_decay_w0, rw_decay_w1, rw_decay_w2, rw_iclr_a0, rw_iclr_a1, rw_iclr_a2, rw_gate_g1, rw_gate_g2, rw_gn_w, rw_gn_b):
    bsz, seq, d = x.shape
    n_ctx = ctx.shape[1]
    hh = MLA_HEADS

    cc = _pad8(jnp.concatenate([c, c_ctx[None, :]], axis=0))
    mod = _adaln_mod(cc, ada_w, ada_b)
    mod_l = [mod[i, :bsz].reshape(bsz, 6, d) for i in range(2)]
    mod_c = [mod[i, bsz:bsz + 1].reshape(1, 6, d) for i in range(2)]

    swap = np.arange(MLA_ROPE) ^ ROPE_FREQ
    wdq = mla_w_dqkv[0]
    rope0 = MLA_Q_LORA + MLA_KV_LORA
    wd = jnp.concatenate([wdq, wdq[:, rope0 + swap]], axis=1).astype(BF16)
    wuq = mla_w_uq[0].reshape(MLA_Q_LORA, hh, MLA_QK)
    wuq = jnp.concatenate([wuq, wuq[:, :, MLA_NOPE + swap]], axis=2).reshape(MLA_Q_LORA, hh * MLA_QK_PAD).astype(BF16)
    wukv = mla_w_ukv[0].astype(BF16)
    gqn, gkn = mla_g_qn[0], mla_g_kn[0]
    gq = jnp.concatenate([gqn, gqn[MLA_NOPE + swap]])[None, :]
    gk = jnp.concatenate([gkn, gkn[MLA_NOPE + swap]])[None, :]
    proj_w = (wd, mla_g_q_lora[0][None, :], mla_g_kv_lora[0][None, :], wuq, wukv, gq, gk)
    tab_l = _rope_table(seq)
    tab_c = jnp.concatenate([jnp.ones((n_ctx, MLA_ROPE), F32), jnp.zeros((n_ctx, MLA_ROPE), F32)], axis=1)
    gmix0 = norm_mix[0][None, :]
    tm_l = min(512, seq)
    tm_c = min(256, n_ctx)
    q_l, k_l, v_l = _mla_proj(x, mod_l[0], gmix0, proj_w, tab_l, tm_l)
    q_c, k_c, v_c = _mla_proj(ctx, mod_c[0], gmix0, proj_w, tab_c, tm_c)
    tk_l = min(8192, seq)
    o_l = _flash(q_l, k_l, v_l, k_c, v_c, min(1024, seq), tk_l, max(tk_l // FLASH_KEYS, 1))
    o_c = _flash(q_c, k_c, v_c, None, None, tm_c, n_ctx, 1)

    w1 = ffn_w1.astype(BF16)
    w3 = ffn_w3.astype(BF16)
    w2 = ffn_w2.astype(BF16)
    tf = 256
    wo0 = mla_w_o[0].astype(BF16)
    dff = w1.shape[2]
    tf_l = dff // 2 if dff % 256 == 0 else tf
    tm_ffn = min(1024, seq)
    x1 = _post_ffn(x, o_l, mod_l[0], wo0, norm_ffn[0][None, :], w1[0], w3[0], w2[0], tm_ffn, tf_l)
    ctx1 = _post_ffn(ctx, o_c, mod_c[0], wo0, norm_ffn[0][None, :], w1[0], w3[0], w2[0], tm_c, tf)

    zpad = RW_GATE_PAD - RW_GATE_LORA
    dw2 = rw_decay_w2[0]
    ia2 = rw_iclr_a2[0]
    zero = jnp.zeros_like(dw2[0])
    feat_w = (
        rw_w_r[0].astype(BF16), rw_w_k[0].astype(BF16), rw_w_v[0].astype(BF16),
        jnp.concatenate([rw_decay_w1[0, 0], rw_decay_w1[0, 1]], axis=1).astype(BF16),
        jnp.concatenate([jnp.concatenate([dw2[0], zero], 1), jnp.concatenate([zero, dw2[1]], 1)], 0).astype(BF16),
        jnp.concatenate([rw_iclr_a1[0, 0], rw_iclr_a1[0, 1]], axis=1).astype(BF16),
        jnp.concatenate([jnp.concatenate([ia2[0], zero], 1), jnp.concatenate([zero, ia2[1]], 1)], 0).astype(BF16),
        jnp.pad(rw_gate_g1[0], ((0, 0), (0, zpad))).astype(BF16),
        jnp.pad(rw_gate_g2[0], ((0, zpad), (0, 0))).astype(BF16),
    )
    head_of = np.arange(d) // RW_HEAD
    hsum = jnp.asarray(head_of[:, None] == head_of[None, :], BF16)
    vec = _pad8(jnp.stack([rw_k_k[0], rw_k_a[0], rw_decay_w0[0, 0], rw_decay_w0[0, 1],
                           rw_iclr_a0[0, 0], rw_iclr_a0[0, 1]]))
    gmix1 = norm_mix[1][None, :]
    tmf = 512
    r_l, v_l2, g_l, kk_l, kd_l, lw_l, b_l = _rwkv_feat(x1, mod_l[1], gmix1, rw_mix[0], vec, feat_w + (hsum,), min(tmf, seq))
    r_c, v_c2, _, kk_c, kd_c, lw_c, b_c = _rwkv_feat(ctx1, mod_c[1], gmix1, rw_mix[0], vec, feat_w + (hsum,), min(tmf, n_ctx))
    gw = WKV_GROUP * RW_HEAD
    s_zero = jnp.zeros((2, bsz, d // gw, gw, gw), F32)
    _, s_ctx = _wkv(r_c, v_c2, kk_c, kd_c, lw_c, b_c, s_zero)
    y, _ = _wkv(r_l, v_l2, kk_l, kd_l, lw_l, b_l, s_ctx)
    ovec = _pad8(jnp.stack([rw_r_k[0].reshape(-1), rw_gn_w[0], rw_gn_b[0]]))
    hred = jnp.asarray(head_of[:, None] == np.arange(128)[None, :], BF16)
    z = _rwkv_out(y, r_l, kd_l, v_l2, g_l, ovec, hred, hred.T, min(256, seq))
    return _post_ffn(x1, z, mod_l[1], rw_w_o[0].astype(BF16), norm_ffn[1][None, :], w1[1], w3[1], w2[1], tm_ffn, tf_l)
```

```python
import functools

import jax
import jax.numpy as jnp
import numpy as np
from jax import lax
from jax.experimental import pallas as pl
from jax.experimental.pallas import tpu as pltpu

F32 = jnp.float32
BF16 = jnp.bfloat16
HIGHEST = lax.Precision.HIGHEST

EPS = 1e-6
GRID_W = 64
MLA_HEADS = 8
MLA_Q_LORA = 384
MLA_KV_LORA = 256
MLA_NOPE = 128
MLA_ROPE = 64
MLA_V = 128
MLA_QK = MLA_NOPE + MLA_ROPE
MLA_QK_PAD = 256
MLA_V_PAD = 256
Q_SCALE = float(np.log2(np.e)) * MLA_QK ** -0.5
ROPE_FREQ = MLA_ROPE // 4
ROPE_BASE = 10000.0
RW_HEAD = 64
RW_GATE_LORA = 160
RW_GATE_PAD = 256
RW_GN_EPS = 64e-5
DECAY_SCALE = float(np.exp(-0.5))
WKV_CHUNK = 64
WKV_GROUP = 4
FLASH_ROWS = 256
FLASH_KEYS = 4096

VMEM_LIMIT_BYTES = 56 * 1024 * 1024


def _params(sem):
    return pltpu.CompilerParams(dimension_semantics=sem, vmem_limit_bytes=VMEM_LIMIT_BYTES)


def _dot(a, b):
    return jnp.dot(a, b, preferred_element_type=F32)


def _dot_nt(a, b):
    return lax.dot_general(a, b, (((1,), (1,)), ((), ())), preferred_element_type=F32)


def _dot_tn(a, b):
    return lax.dot_general(a, b, (((0,), (0,)), ((), ())), preferred_element_type=F32)


def _norm_mod(x, g, shift, scale):
    ms = jnp.mean(x * x, axis=-1, keepdims=True)
    return (x * lax.rsqrt(ms + EPS) * g) * (1.0 + scale) + shift


def _rms(x, g):
    ms = jnp.mean(x * x, axis=-1, keepdims=True)
    return x * lax.rsqrt(ms + EPS) * g


def _full(shape):
    n = len(shape)
    return pl.BlockSpec(shape, lambda *_: (0,) * n)


def _mod_kernel(cc_ref, w_ref, b_ref, o_ref):
    cc = cc_ref[...]
    s = cc * jax.nn.sigmoid(cc)
    o_ref[0] = jnp.dot(s, w_ref[0], precision=HIGHEST, preferred_element_type=F32) + b_ref[0]


def _adaln_mod(cc, ada_w, ada_b):
    depth, d, n = ada_w.shape
    tn = 1536
    rows = cc.shape[0]
    return pl.pallas_call(
        _mod_kernel,
        grid=(depth, n // tn),
        in_specs=[
            pl.BlockSpec((rows, d), lambda i, j: (0, 0)),
            pl.BlockSpec((1, d, tn), lambda i, j: (i, 0, j)),
            pl.BlockSpec((1, 1, tn), lambda i, j: (i, 0, j)),
        ],
        out_specs=pl.BlockSpec((1, rows, tn), lambda i, j: (i, 0, j)),
        out_shape=jax.ShapeDtypeStruct((depth, rows, n), F32),
        compiler_params=_params(("parallel", "parallel")),
        name="adaln_mod",
    )(cc, ada_w, ada_b.reshape(depth, 1, n))


def _mla_proj_kernel(x_ref, mod_ref, gmix_ref, wd_ref, glq_ref, glkv_ref, wuq_ref, wukv_ref,
                     gains_ref, tab_ref, q_ref, k_ref, v_ref):
    mod = mod_ref[0]
    h = _norm_mod(x_ref[0], gmix_ref[...], mod[0:1], mod[1:2])
    down = _dot(h.astype(BF16), wd_ref[...])
    n_lat = MLA_Q_LORA + MLA_KV_LORA
    cq = _rms(down[:, :MLA_Q_LORA], glq_ref[...]).astype(BF16)
    ckv = _rms(down[:, MLA_Q_LORA:n_lat], glkv_ref[...]).astype(BF16)
    kr2 = down[:, n_lat:n_lat + 128]
    kw2 = down[:, n_lat + 128:]
    tab = tab_ref[...]
    lo = lax.broadcasted_iota(jnp.int32, (1, 128), 1) < MLA_ROPE
    tab_r = pltpu.roll(tab, MLA_ROPE, 1)
    cos2 = jnp.where(lo, tab, tab_r)
    sin2 = jnp.where(lo, tab_r, tab)
    g = gains_ref[...]
    k_rot2 = kr2 * g[4:5] * cos2 + kw2 * g[5:6] * sin2
    kr_sq = jnp.where(lo, kr2 * kr2, 0.0)
    ones_col = (lax.broadcasted_iota(jnp.int32, (x_ref.shape[1], 128), 1) == 0).astype(BF16)
    for pair in range(MLA_HEADS // 2):
        qp = _dot(cq, wuq_ref[:, pair * 512:(pair + 1) * 512])
        tails = qp[:, 256:384]
        rot2 = tails * g[1:2] * cos2 + qp[:, 384:512] * g[2:3] * sin2
        tails_sq = tails * tails
        for j in range(2):
            hd = 2 * pair + j
            own = lo if j == 0 else jnp.logical_not(lo)
            qn = qp[:, j * 128:(j + 1) * 128]
            ssq = jnp.sum(qn * qn + jnp.where(own, tails_sq, 0.0), axis=-1, keepdims=True)
            rq = lax.rsqrt(ssq * (1.0 / MLA_QK) + EPS) * Q_SCALE
            q_ref[0, hd, :, :128] = (qn * g[0:1] * rq).astype(BF16)
            q_ref[0, hd, :, 128:] = (jnp.where(own, rot2, 0.0) * rq).astype(BF16)
            kvh = _dot(ckv, wukv_ref[:, hd * MLA_QK_PAD:(hd + 1) * MLA_QK_PAD])
            kn = kvh[:, :128]
            rk = lax.rsqrt(jnp.sum(kn * kn + kr_sq, axis=-1, keepdims=True) * (1.0 / MLA_QK) + EPS)
            k_ref[0, hd, :, :128] = (kn * g[3:4] * rk).astype(BF16)
            k_ref[0, hd, :, 128:] = (jnp.where(own, k_rot2, 0.0) * rk).astype(BF16)
            v_ref[0, hd, :, :MLA_V] = kvh[:, 128:].astype(BF16)
            v_ref[0, hd, :, MLA_V:] = ones_col


def _mla_proj(x, mod, gmix, wts, tab, tm):
    b, l, d = x.shape
    hh = MLA_HEADS
    per_batch_mod = mod.shape[0] > 1
    mod_map = (lambda bi, i: (bi, 0, 0)) if per_batch_mod else (lambda bi, i: (0, 0, 0))
    wd, glq, glkv, wuq, wukv, gains = wts
    return pl.pallas_call(
        _mla_proj_kernel,
        grid=(b, l // tm),
        in_specs=[
            pl.BlockSpec((1, tm, d), lambda bi, i: (bi, i, 0)),
            pl.BlockSpec((1, 6, d), mod_map),
            _full(gmix.shape), _full(wd.shape), _full(glq.shape), _full(glkv.shape),
            _full(wuq.shape), _full(wukv.shape), _full(gains.shape),
            pl.BlockSpec((tm, 128), lambda bi, i: (i, 0)),
        ],
        out_specs=[
            pl.BlockSpec((1, hh, tm, MLA_QK_PAD), lambda bi, i: (bi, 0, i, 0)),
            pl.BlockSpec((1, hh, tm, MLA_QK_PAD), lambda bi, i: (bi, 0, i, 0)),
            pl.BlockSpec((1, hh, tm, MLA_V_PAD), lambda bi, i: (bi, 0, i, 0)),
        ],
        out_shape=[
            jax.ShapeDtypeStruct((b, hh, l, MLA_QK_PAD), BF16),
            jax.ShapeDtypeStruct((b, hh, l, MLA_QK_PAD), BF16),
            jax.ShapeDtypeStruct((b, hh, l, MLA_V_PAD), BF16),
        ],
        compiler_params=_params(("parallel", "parallel")),
        name="mla_proj",
    )(x, mod, gmix, wd, glq, glkv, wuq, wukv, gains, tab)


def _flash_kernel(*refs, has_ctx, n_sub):
    if has_ctx:
        q_ref, k_ref, v_ref, kc_ref, vc_ref, o_ref, m_sc, acc_sc = refs
    else:
        q_ref, k_ref, v_ref, o_ref, m_sc, acc_sc = refs
    j = pl.program_id(3)
    tr = min(FLASH_ROWS, q_ref.shape[2])
    n_row = q_ref.shape[2] // tr

    def attend(kv_ref, vv_ref, n_chunk):
        ts = kv_ref.shape[2] // n_chunk
        pairs = [(c, r) for c in range(n_chunk) for r in range(n_row)]
        m_run = [m_sc[r * tr:(r + 1) * tr] for r in range(n_row)]
        s_of, p_of = {}, {}
        for t in range(len(pairs) + 2):
            if t < len(pairs):
                c, r = pairs[t]
                s_of[t] = _dot_nt(q_ref[0, 0, r * tr:(r + 1) * tr], kv_ref[0, 0, c * ts:(c + 1) * ts])
            if 0 <= t - 1 < len(pairs):
                c, r = pairs[t - 1]
                s = s_of.pop(t - 1)
                m_new = jnp.maximum(m_run[r], jnp.max(s, axis=-1, keepdims=True))
                p_of[t - 1] = (jnp.exp2(s - m_new).astype(BF16), jnp.exp2(m_run[r] - m_new))
                m_run[r] = m_new
            if 0 <= t - 2 < len(pairs):
                c, r = pairs[t - 2]
                p, alpha = p_of.pop(t - 2)
                rows = slice(r * tr, (r + 1) * tr)
                acc_sc[rows] = alpha * acc_sc[rows] + _dot(p, vv_ref[0, 0, c * ts:(c + 1) * ts])
        for r in range(n_row):
            m_sc[r * tr:(r + 1) * tr] = m_run[r]

    @pl.when(j == 0)
    def _():
        m_sc[...] = jnp.full(m_sc.shape, -1e30, F32)
        acc_sc[...] = jnp.zeros(acc_sc.shape, F32)
        if has_ctx:
            attend(kc_ref, vc_ref, 1)

    attend(k_ref, v_ref, n_sub)

    @pl.when(j == pl.num_programs(3) - 1)
    def _():
        acc = acc_sc[...]
        o_ref[0] = (acc[:, :MLA_V] / acc[:, MLA_V:MLA_V + 1]).astype(o_ref.dtype)


def _flash(q, k, v, kc, vc, tq, tk, n_sub):
    b, hh, sq, dq = q.shape
    lk = k.shape[2]
    dv = v.shape[3]
    has_ctx = kc is not None
    in_specs = [
        pl.BlockSpec((1, 1, tq, dq), lambda bi, h, i, j: (bi, h, i, 0)),
        pl.BlockSpec((1, 1, tk, dq), lambda bi, h, i, j: (bi, h, j, 0)),
        pl.BlockSpec((1, 1, tk, dv), lambda bi, h, i, j: (bi, h, j, 0)),
    ]
    args = [q, k, v]
    if has_ctx:
        lc = kc.shape[2]
        in_specs += [
            pl.BlockSpec((1, 1, lc, dq), lambda bi, h, i, j: (bi, h, 0, 0)),
            pl.BlockSpec((1, 1, lc, dv), lambda bi, h, i, j: (bi, h, 0, 0)),
        ]
        args += [kc, vc]
    return pl.pallas_call(
        functools.partial(_flash_kernel, has_ctx=has_ctx, n_sub=n_sub),
        grid=(b, hh, sq // tq, lk // tk),
        in_specs=in_specs,
        out_specs=pl.BlockSpec((1, tq, MLA_V), lambda bi, h, i, j: (bi, i, h)),
        out_shape=jax.ShapeDtypeStruct((b, sq, hh * MLA_V), BF16),
        scratch_shapes=[pltpu.VMEM((tq, 1), F32), pltpu.VMEM((tq, dv), F32)],
        compiler_params=_params(("parallel", "parallel", "parallel", "arbitrary")),
        name="mla_flash_ctx" if has_ctx else "mla_flash",
    )(*args)


def _post_ffn_kernel(x_ref, z_ref, mod_ref, wo_ref, gffn_ref, w1_ref, w3_ref, w2_ref, o_ref,
                     h_sc, acc_sc):
    f = pl.program_id(2)
    mod = mod_ref[0]

    @pl.when(f == 0)
    def _():
        x1 = x_ref[0] + mod[2:3] * _dot(z_ref[0], wo_ref[...])
        o_ref[0] = x1
        h_sc[...] = _norm_mod(x1, gffn_ref[...], mod[3:4], mod[4:5]).astype(BF16)
        acc_sc[...] = jnp.zeros(acc_sc.shape, F32)

    h = h_sc[...]
    a = _dot(h, w1_ref[...])
    g = (a * jax.nn.sigmoid(a)) * _dot(h, w3_ref[...])
    acc_sc[...] += _dot(g.astype(BF16), w2_ref[...])

    @pl.when(f == pl.num_programs(2) - 1)
    def _():
        o_ref[0] = o_ref[0] + mod[5:6] * acc_sc[...]


def _post_ffn(x, z, mod, wo, gffn, w1, w3, w2, tm, tf):
    b, l, d = x.shape
    dff = w1.shape[1]
    per_batch_mod = mod.shape[0] > 1
    mod_map = (lambda bi, i, f: (bi, 0, 0)) if per_batch_mod else (lambda bi, i, f: (0, 0, 0))
    return pl.pallas_call(
        _post_ffn_kernel,
        grid=(b, l // tm, dff // tf),
        in_specs=[
            pl.BlockSpec((1, tm, d), lambda bi, i, f: (bi, i, 0)),
            pl.BlockSpec((1, tm, d), lambda bi, i, f: (bi, i, 0)),
            pl.BlockSpec((1, 6, d), mod_map),
            pl.BlockSpec((d, d), lambda bi, i, f: (0, 0)),
            pl.BlockSpec((1, d), lambda bi, i, f: (0, 0)),
            pl.BlockSpec((d, tf), lambda bi, i, f: (0, f)),
            pl.BlockSpec((d, tf), lambda bi, i, f: (0, f)),
            pl.BlockSpec((tf, d), lambda bi, i, f: (f, 0)),
        ],
        out_specs=pl.BlockSpec((1, tm, d), lambda bi, i, f: (bi, i, 0)),
        out_shape=jax.ShapeDtypeStruct((b, l, d), F32),
        scratch_shapes=[pltpu.VMEM((tm, d), BF16), pltpu.VMEM((tm, d), F32)],
        compiler_params=_params(("parallel", "parallel", "arbitrary")),
        name="post_ffn",
    )(x, z, mod, wo, gffn, w1, w3, w2)


def _rwkv_feat_kernel(x_ref, xp_ref, xn_ref, mod_ref, gmix_ref, mix_ref, vec_ref, wr_ref, wk_ref, wv_ref,
                      dw1_ref, dw2_ref, ia1_ref, ia2_ref, g1_ref, g2_ref, hsum_ref,
                      r_ref, v_ref, g_ref, kk_ref, kd_ref, lw_ref, b_ref):
    i = pl.program_id(1)
    nt = pl.num_programs(1)
    mod = mod_ref[0]
    gmix = gmix_ref[...]
    tm, d = x_ref.shape[1], x_ref.shape[2]
    h = _norm_mod(x_ref[0], gmix, mod[0:1], mod[1:2])
    hp = _norm_mod(xp_ref[0], gmix, mod[0:1], mod[1:2])[7:8]
    hn = _norm_mod(xn_ref[0], gmix, mod[0:1], mod[1:2])[0:1]
    hp = jnp.where(i > 0, hp, 0.0)
    hn = jnp.where(i < nt - 1, hn, 0.0)
    row = lax.broadcasted_iota(jnp.int32, (tm, 1), 0)
    prev = jnp.where(row == 0, hp, pltpu.roll(h, 1, 0))
    nxt = jnp.where(row == tm - 1, hn, pltpu.roll(h, tm - 1, 0))
    xx = 0.5 * (prev + nxt) - h
    mix = mix_ref[...]
    vec = vec_ref[...]

    def mixed(jj):
        return (h + xx * mix[jj:jj + 1]).astype(BF16)

    r_ref[0] = _dot(mixed(0), wr_ref[...]).astype(r_ref.dtype)
    k = _dot(mixed(2), wk_ref[...])
    v_ref[0] = _dot(mixed(3), wv_ref[...]).astype(v_ref.dtype)
    g_ref[0] = _dot(jax.nn.sigmoid(_dot(mixed(5), g1_ref[...])).astype(BF16), g2_ref[...]).astype(g_ref.dtype)
    dd = _dot(jnp.tanh(_dot(mixed(1), dw1_ref[...])).astype(BF16), dw2_ref[...])
    aa = _dot(_dot(mixed(4), ia1_ref[...]).astype(BF16), ia2_ref[...])
    kkr = k * vec[0:1]
    kk = kkr * lax.rsqrt(_dot((kkr * kkr).astype(BF16), hsum_ref[...]) + 1e-12)
    kk_ref[0] = kk.astype(kk_ref.dtype)
    for dr in range(2):
        lw_ref[dr, 0] = -DECAY_SCALE * jax.nn.sigmoid(vec[2 + dr:3 + dr] + dd[:, dr * d:(dr + 1) * d])
        a = jax.nn.sigmoid(vec[4 + dr:5 + dr] + aa[:, dr * d:(dr + 1) * d])
        kd_ref[dr, 0] = (k * (1.0 + (a - 1.0) * vec[1:2])).astype(kd_ref.dtype)
        b_ref[dr, 0] = (kk * a).astype(b_ref.dtype)


def _rwkv_feat(x, mod, gmix, mix, vec, wts, tm):
    b, l, d = x.shape
    per_batch_mod = mod.shape[0] > 1
    mod_map = (lambda bi, i: (bi, 0, 0)) if per_batch_mod else (lambda bi, i: (0, 0, 0))
    t8 = tm // 8
    last8 = l // 8 - 1
    tok = pl.BlockSpec((1, tm, d), lambda bi, i: (bi, i, 0))
    tok2 = pl.BlockSpec((2, 1, tm, d), lambda bi, i: (0, bi, i, 0))
    one = jax.ShapeDtypeStruct((b, l, d), BF16)
    two = jax.ShapeDtypeStruct((2, b, l, d), BF16)
    two_f32 = jax.ShapeDtypeStruct((2, b, l, d), F32)
    return pl.pallas_call(
        _rwkv_feat_kernel,
        grid=(b, l // tm),
        in_specs=[
            tok,
            pl.BlockSpec((1, 8, d), lambda bi, i: (bi, jnp.maximum(i * t8 - 1, 0), 0)),
            pl.BlockSpec((1, 8, d), lambda bi, i: (bi, jnp.minimum((i + 1) * t8, last8), 0)),
            pl.BlockSpec((1, 6, d), mod_map),
            _full(gmix.shape), _full(mix.shape), _full(vec.shape),
        ] + [_full(w.shape) for w in wts],
        out_specs=[tok, tok, tok, tok, tok2, tok2, tok2],
        out_shape=[one, one, one, one, two, two_f32, two],
        compiler_params=_params(("parallel", "parallel")),
        name="rwkv_feat",
    )(x, x, x, mod, gmix, mix, vec, *wts)


def _wkv_kernel(r_ref, v_ref, kk_ref, kd_ref, lw_ref, b_ref, s0_ref, y_ref, sT_ref, st_sc):
    dr = pl.program_id(0)
    s = pl.program_id(1)
    nb = r_ref.shape[0]
    c = WKV_CHUNK
    n = RW_HEAD
    gw = WKV_GROUP * n
    d = r_ref.shape[2]
    groups = range(d // gw)

    @pl.when(s == 0)
    def _():
        st_sc[...] = s0_ref[0]

    sign = 1 - 2 * dr
    tt = lax.broadcasted_iota(jnp.int32, (c, c), 0)
    ss = lax.broadcasted_iota(jnp.int32, (c, c), 1)
    incl = (tt - ss) * sign >= 0
    t4 = lax.broadcasted_iota(jnp.int32, (c, gw), 0)
    s4 = lax.broadcasted_iota(jnp.int32, (c, gw), 1) % n
    ahead4 = (t4 - s4) * sign
    incl4 = ahead4 >= 0
    strict4 = ahead4 > 0
    eye4 = (t4 == s4).astype(F32)
    lane_head = lax.broadcasted_iota(jnp.int32, (c, 2 * n), 1) // n
    row_blk = lax.broadcasted_iota(jnp.int32, (gw, gw), 0) // n
    col_blk = lax.broadcasted_iota(jnp.int32, (gw, gw), 1) // n
    diag_blk = row_blk == col_blk

    def bdiag(x):
        xb = x.astype(BF16)
        zero = jnp.zeros((c, 2 * n), BF16)
        rows = []
        for hd in range(WKV_GROUP):
            tile = xb[:, (hd // 2) * 2 * n:(hd // 2 + 1) * 2 * n]
            tile = jnp.where(lane_head == hd % 2, tile, zero)
            rows.append(jnp.concatenate([tile, zero] if hd < 2 else [zero, tile], axis=1))
        return jnp.concatenate(rows, axis=0)

    tri = incl.astype(BF16)
    last = jnp.where(dr == 0, c - 1, 0)
    rowsel = lax.broadcasted_iota(jnp.int32, (c, 1), 0) == last
    at, rt, bt, kt, bw, kw, v, w_end = [], [], [], [], [], [], [], []
    for bi in range(nb):
        lw = lw_ref[0, bi]
        lw_hi = lw.astype(BF16)
        lw_lo = (lw - lw_hi.astype(F32)).astype(BF16)
        cum = _dot(tri, lw_hi) + _dot(tri, lw_lo)
        e_pos = jnp.exp(cum)
        e_neg = jnp.exp(-cum)
        we = jnp.sum(jnp.where(rowsel, e_pos, 0.0), axis=0, keepdims=True)
        rt.append(r_ref[bi].astype(F32) * e_pos)
        at.append(-kk_ref[bi].astype(F32) * jnp.exp(cum - lw))
        bt.append(b_ref[0, bi].astype(F32) * e_neg)
        kt.append(kd_ref[0, bi].astype(F32) * e_neg)
        bw.append(bt[bi] * we)
        kw.append(kt[bi] * we)
        v.append(v_ref[bi].astype(F32))
        w_end.append(we)

    units = [(bi, g) for bi in range(nb) for g in groups]
    nu = range(len(units))
    sl = [slice(g * gw, (g + 1) * gw) for _, g in units]
    bix = [bi for bi, _ in units]
    s0 = [st_sc[bi, g] for bi, g in units]
    ar = [jnp.concatenate([at[bix[i]][:, sl[i]], rt[bix[i]][:, sl[i]]], axis=0).astype(BF16) for i in nu]
    v_bd = [bdiag(v[bix[i]][:, sl[i]]) for i in nu]
    pb = [_dot_nt(ar[i], bdiag(bt[bix[i]][:, sl[i]])) for i in nu]
    pk = [_dot_nt(ar[i], bdiag(kt[bix[i]][:, sl[i]])) for i in nu]
    ars = [_dot_nt(ar[i], s0[i].astype(BF16)) for i in nu]
    l_ab = [jnp.where(strict4, pb[i][:c], 0.0) for i in nu]
    lakv = [_dot(jnp.where(strict4, pk[i][:c], 0.0).astype(BF16), v_bd[i]) for i in nu]
    m = [jnp.concatenate([jnp.where(incl4, pb[i][c:], 0.0), jnp.where(incl4, pk[i][c:], 0.0)],
                         axis=1).astype(BF16) for i in nu]
    tinv = [eye4 + l_ab[i] for i in nu]
    lp = [_dot(l_ab[i].astype(BF16), bdiag(l_ab[i])) for i in nu]
    for step in range(5):
        lp_b = [lp[i].astype(BF16) for i in nu]
        upd = [_dot(lp_b[i], bdiag(tinv[i])) for i in nu]
        if step < 4:
            lp = [_dot(lp_b[i], bdiag(lp[i])) for i in nu]
        tinv = [tinv[i] + upd[i] for i in nu]
    x = [ars[i][:c] + lakv[i] for i in nu]
    u = [_dot(tinv[i].astype(BF16), bdiag(x[i])) for i in nu]
    yg = [ars[i][c:] + _dot(m[i], jnp.concatenate([bdiag(u[i]), v_bd[i]], axis=0)) for i in nu]
    uv = [jnp.concatenate([u[i], v[bix[i]][:, sl[i]]], axis=0).astype(BF16) for i in nu]
    bkw = [jnp.concatenate([bw[bix[i]][:, sl[i]], kw[bix[i]][:, sl[i]]], axis=0).astype(BF16) for i in nu]
    s1 = [jnp.where(diag_blk, s0[i] * w_end[bix[i]][:, sl[i]] + _dot_tn(uv[i], bkw[i]), 0.0) for i in nu]
    for i, (bi, g) in enumerate(units):
        y_ref[0, bi, :, sl[i]] = yg[i]
        st_sc[bi, g] = s1[i]

    @pl.when(s == pl.num_programs(1) - 1)
    def _():
        sT_ref[0] = st_sc[...]


def _wkv(r, v, kk, kd, lw, bb, s0):
    b, l, d = r.shape
    c = WKV_CHUNK
    nc = l // c
    gw = WKV_GROUP * RW_HEAD
    ng = d // gw
    assert WKV_GROUP == 4 and RW_HEAD == 64 and d % gw == 0

    def cidx(dr, s):
        return s + dr * (nc - 1 - 2 * s)

    tok = pl.BlockSpec((b, c, d), lambda dr, s: (0, cidx(dr, s), 0))
    tok2 = pl.BlockSpec((1, b, c, d), lambda dr, s: (dr, 0, cidx(dr, s), 0))
    st = pl.BlockSpec((1, b, ng, gw, gw), lambda dr, s: (dr, 0, 0, 0, 0))
    return pl.pallas_call(
        _wkv_kernel,
        grid=(2, nc),
        in_specs=[tok, tok, tok, tok2, tok2, tok2, st],
        out_specs=[tok2, st],
        out_shape=[jax.ShapeDtypeStruct((2, b, l, d), F32),
                   jax.ShapeDtypeStruct((2, b, ng, gw, gw), F32)],
        scratch_shapes=[pltpu.VMEM((b, ng, gw, gw), F32)],
        compiler_params=_params(("parallel", "arbitrary")),
        name="wkv7",
    )(r, v, kk, kd, lw, bb, s0)


def _rwkv_out_kernel(y_ref, r_ref, kd_ref, v_ref, g_ref, vec_ref, hred_ref, hexp_ref, z_ref):
    vec = vec_ref[...]
    inv_n = 1.0 / RW_HEAD

    def head_sum(t):
        return _dot(_dot(t.astype(BF16), hred_ref[...]).astype(BF16), hexp_ref[...])

    y = y_ref[0, 0] + y_ref[1, 0]
    mu = head_sum(y) * inv_n
    yc = y - mu
    var = head_sum(yc * yc) * inv_n
    yn = yc * lax.rsqrt(var + RW_GN_EPS) * vec[1:2] + vec[2:3]
    rk = r_ref[0].astype(F32) * (kd_ref[0, 0].astype(F32) + kd_ref[1, 0].astype(F32)) * vec[0:1]
    bonus = head_sum(rk) * v_ref[0].astype(F32)
    z_ref[0] = ((yn + bonus) * g_ref[0].astype(F32)).astype(z_ref.dtype)


def _rwkv_out(y, r, kd, v, g, vec, hred, hexp, tm):
    b, l, d = r.shape
    tok = pl.BlockSpec((1, tm, d), lambda bi, i: (bi, i, 0))
    tok2 = pl.BlockSpec((2, 1, tm, d), lambda bi, i: (0, bi, i, 0))
    return pl.pallas_call(
        _rwkv_out_kernel,
        grid=(b, l // tm),
        in_specs=[tok2, tok, tok2, tok, tok, _full(vec.shape), _full(hred.shape), _full(hexp.shape)],
        out_specs=tok,
        out_shape=jax.ShapeDtypeStruct((b, l, d), BF16),
        compiler_params=_params(("parallel", "parallel")),
        name="rwkv_out",
    )(y, r, kd, v, g, vec, hred, hexp)


def _rope_table(n_tokens):
    rows = n_tokens // GRID_W
    row = jnp.broadcast_to(jnp.arange(rows, dtype=F32)[:, None], (rows, GRID_W)).reshape(-1)
    col = jnp.broadcast_to(jnp.arange(GRID_W, dtype=F32)[None, :], (rows, GRID_W)).reshape(-1)
    inv = ROPE_BASE ** (-jnp.arange(ROPE_FREQ, dtype=F32) / ROPE_FREQ)
    ar, ac = row[:, None] * inv, col[:, None] * inv
    cr, sr, cc, sc = jnp.cos(ar), jnp.sin(ar), jnp.cos(ac), jnp.sin(ac)
    return jnp.concatenate([cr, cr, cc, cc, -sr, sr, -sc, sc], axis=1)


def _pad8(x):
    return jnp.pad(x, ((0, 8 - x.shape[0]), (0, 0)))


def kernel(x, c, ctx, c_ctx, ada_w, ada_b, norm_mix, norm_ffn, ffn_w1, ffn_w3, ffn_w2, mla_w_dqkv, mla_g_q_lora, mla_g_kv_lora, mla_w_uq, mla_w_ukv, mla_g_qn, mla_g_kn, mla_w_o, rw_mix, rw_w_r, rw_w_k, rw_w_v, rw_w_o, rw_k_k, rw_k_a, rw_r_k, rw_decay_w0, rw_decay_w1, rw_decay_w2, rw_iclr_a0, rw_iclr_a1, rw_iclr_a2, rw_gate_g1, rw_gate_g2, rw_gn_w, rw_gn_b):
    bsz, seq, d = x.shape
    n_ctx = ctx.shape[1]
    hh = MLA_HEADS

    cc = _pad8(jnp.concatenate([c, c_ctx[None, :]], axis=0))
    mod = _adaln_mod(cc, ada_w, ada_b)
    mod_l = [mod[i, :bsz].reshape(bsz, 6, d) for i in range(2)]
    mod_c = [mod[i, bsz:bsz + 1].reshape(1, 6, d) for i in range(2)]

    swap = np.arange(MLA_ROPE) ^ ROPE_FREQ
    wdq = mla_w_dqkv[0]
    rope0 = MLA_Q_LORA + MLA_KV_LORA
    kr_w, ks_w = wdq[:, rope0:], wdq[:, rope0 + swap]
    wd = jnp.concatenate([wdq[:, :rope0], kr_w, kr_w, ks_w, ks_w], axis=1).astype(BF16)
    wuq = mla_w_uq[0].reshape(MLA_Q_LORA, hh, MLA_QK)
    nope, tail = wuq[:, :, :MLA_NOPE], wuq[:, :, MLA_NOPE:]
    tail_sw = tail[:, :, swap]
    wuq = jnp.concatenate([jnp.concatenate([nope[:, a], nope[:, a + 1], tail[:, a], tail[:, a + 1],
                                            tail_sw[:, a], tail_sw[:, a + 1]], axis=1)
                           for a in range(0, hh, 2)], axis=1).astype(BF16)
    wukv = mla_w_ukv[0].astype(BF16)

    def gain_rows(gn):
        tail_g = gn[MLA_NOPE:]
        return [gn[:MLA_NOPE], jnp.concatenate([tail_g, tail_g]), jnp.concatenate([tail_g[swap], tail_g[swap]])]

    gains = _pad8(jnp.stack(gain_rows(mla_g_qn[0]) + gain_rows(mla_g_kn[0])))
    proj_w = (wd, mla_g_q_lora[0][None, :], mla_g_kv_lora[0][None, :], wuq, wukv, gains)
    tab_l = _rope_table(seq)
    tab_c = jnp.concatenate([jnp.ones((n_ctx, MLA_ROPE), F32), jnp.zeros((n_ctx, MLA_ROPE), F32)], axis=1)
    gmix0 = norm_mix[0][None, :]
    tm_l = min(512, seq)
    tm_c = min(256, n_ctx)
    q_l, k_l, v_l = _mla_proj(x, mod_l[0], gmix0, proj_w, tab_l, tm_l)
    q_c, k_c, v_c = _mla_proj(ctx, mod_c[0], gmix0, proj_w, tab_c, tm_c)
    tk_l = min(8192, seq)
    o_l = _flash(q_l, k_l, v_l, k_c, v_c, min(1024, seq), tk_l, max(tk_l // FLASH_KEYS, 1))
    o_c = _flash(q_c, k_c, v_c, None, None, tm_c, n_ctx, 1)

    w1 = ffn_w1.astype(BF16)
    w3 = ffn_w3.astype(BF16)
    w2 = ffn_w2.astype(BF16)
    tf = 256
    wo0 = mla_w_o[0].astype(BF16)
    dff = w1.shape[2]
    tf_l = dff // 2 if dff % 256 == 0 else tf
    tm_ffn = min(1024, seq)
    x1 = _post_ffn(x, o_l, mod_l[0], wo0, norm_ffn[0][None, :], w1[0], w3[0], w2[0], tm_ffn, tf_l)
    ctx1 = _post_ffn(ctx, o_c, mod_c[0], wo0, norm_ffn[0][None, :], w1[0], w3[0], w2[0], tm_c, tf)

    zpad = RW_GATE_PAD - RW_GATE_LORA
    dw2 = rw_decay_w2[0]
    ia2 = rw_iclr_a2[0]
    zero = jnp.zeros_like(dw2[0])
    feat_w = (
        rw_w_r[0].astype(BF16), rw_w_k[0].astype(BF16), rw_w_v[0].astype(BF16),
        jnp.concatenate([rw_decay_w1[0, 0], rw_decay_w1[0, 1]], axis=1).astype(BF16),
        jnp.concatenate([jnp.concatenate([dw2[0], zero], 1), jnp.concatenate([zero, dw2[1]], 1)], 0).astype(BF16),
        jnp.concatenate([rw_iclr_a1[0, 0], rw_iclr_a1[0, 1]], axis=1).astype(BF16),
        jnp.concatenate([jnp.concatenate([ia2[0], zero], 1), jnp.concatenate([zero, ia2[1]], 1)], 0).astype(BF16),
        jnp.pad(rw_gate_g1[0], ((0, 0), (0, zpad))).astype(BF16),
        jnp.pad(rw_gate_g2[0], ((0, zpad), (0, 0))).astype(BF16),
    )
    head_of = np.arange(d) // RW_HEAD
    hsum = jnp.asarray(head_of[:, None] == head_of[None, :], BF16)
    vec = _pad8(jnp.stack([rw_k_k[0], rw_k_a[0], rw_decay_w0[0, 0], rw_decay_w0[0, 1],
                           rw_iclr_a0[0, 0], rw_iclr_a0[0, 1]]))
    gmix1 = norm_mix[1][None, :]
    tmf = 512
    r_l, v_l2, g_l, kk_l, kd_l, lw_l, b_l = _rwkv_feat(x1, mod_l[1], gmix1, rw_mix[0], vec, feat_w + (hsum,), min(tmf, seq))
    r_c, v_c2, _, kk_c, kd_c, lw_c, b_c = _rwkv_feat(ctx1, mod_c[1], gmix1, rw_mix[0], vec, feat_w + (hsum,), min(tmf, n_ctx))
    gw = WKV_GROUP * RW_HEAD
    s_zero = jnp.zeros((2, bsz, d // gw, gw, gw), F32)
    _, s_ctx = _wkv(r_c, v_c2, kk_c, kd_c, lw_c, b_c, s_zero)
    y, _ = _wkv(r_l, v_l2, kk_l, kd_l, lw_l, b_l, s_ctx)
    ovec = _pad8(jnp.stack([rw_r_k[0].reshape(-1), rw_gn_w[0], rw_gn_b[0]]))
    hred = jnp.asarray(head_of[:, None] == np.arange(128)[None, :], BF16)
    z = _rwkv_out(y, r_l, kd_l, v_l2, g_l, ovec, hred, hred.T, min(256, seq))
    return _post_ffn(x1, z, mod_l[1], rw_w_o[0].astype(BF16), norm_ffn[1][None, :], w1[1], w3[1], w2[1], tm_ffn, tf_l)
```

```python
import functools

import jax
import jax.numpy as jnp
import numpy as np
from jax import lax
from jax.experimental import pallas as pl
from jax.experimental.pallas import tpu as pltpu

F32 = jnp.float32
BF16 = jnp.bfloat16
HIGHEST = lax.Precision.HIGHEST

EPS = 1e-6
GRID_W = 64
MLA_HEADS = 8
MLA_Q_LORA = 384
MLA_KV_LORA = 256
MLA_NOPE = 128
MLA_ROPE = 64
MLA_V = 128
MLA_QK = MLA_NOPE + MLA_ROPE
MLA_QK_PAD = 256
MLA_V_PAD = 256
Q_SCALE = float(np.log2(np.e)) * MLA_QK ** -0.5
ROPE_FREQ = MLA_ROPE // 4
ROPE_BASE = 10000.0
RW_HEAD = 64
RW_GATE_LORA = 160
RW_GATE_PAD = 256
RW_GN_EPS = 64e-5
DECAY_SCALE = float(np.exp(-0.5))
WKV_CHUNK = 64
WKV_GROUP = 4
FLASH_ROWS = 256
FLASH_KEYS = 8192

VMEM_LIMIT_BYTES = 56 * 1024 * 1024


def _params(sem):
    return pltpu.CompilerParams(dimension_semantics=sem, vmem_limit_bytes=VMEM_LIMIT_BYTES)


def _dot(a, b):
    return jnp.dot(a, b, preferred_element_type=F32)


def _dot_nt(a, b):
    return lax.dot_general(a, b, (((1,), (1,)), ((), ())), preferred_element_type=F32)


def _dot_tn(a, b):
    return lax.dot_general(a, b, (((0,), (0,)), ((), ())), preferred_element_type=F32)


def _norm_mod(x, g, shift, scale):
    ms = jnp.mean(x * x, axis=-1, keepdims=True)
    return (x * lax.rsqrt(ms + EPS) * g) * (1.0 + scale) + shift


def _rms(x, g):
    ms = jnp.mean(x * x, axis=-1, keepdims=True)
    return x * lax.rsqrt(ms + EPS) * g


def _full(shape):
    n = len(shape)
    return pl.BlockSpec(shape, lambda *_: (0,) * n)


def _mod_kernel(cc_ref, w_ref, b_ref, o_ref):
    cc = cc_ref[...]
    s = cc * jax.nn.sigmoid(cc)
    o_ref[0] = jnp.dot(s, w_ref[0], precision=HIGHEST, preferred_element_type=F32) + b_ref[0]


def _adaln_mod(cc, ada_w, ada_b):
    depth, d, n = ada_w.shape
    tn = 1536
    rows = cc.shape[0]
    return pl.pallas_call(
        _mod_kernel,
        grid=(depth, n // tn),
        in_specs=[
            pl.BlockSpec((rows, d), lambda i, j: (0, 0)),
            pl.BlockSpec((1, d, tn), lambda i, j: (i, 0, j)),
            pl.BlockSpec((1, 1, tn), lambda i, j: (i, 0, j)),
        ],
        out_specs=pl.BlockSpec((1, rows, tn), lambda i, j: (i, 0, j)),
        out_shape=jax.ShapeDtypeStruct((depth, rows, n), F32),
        compiler_params=_params(("parallel", "parallel")),
        name="adaln_mod",
    )(cc, ada_w, ada_b.reshape(depth, 1, n))


def _mla_proj_kernel(x_ref, mod_ref, gmix_ref, wd_ref, glq_ref, glkv_ref, wuq_ref, wukv_ref,
                     gains_ref, tab_ref, q_ref, k_ref, v_ref):
    mod = mod_ref[0]
    h = _norm_mod(x_ref[0], gmix_ref[...], mod[0:1], mod[1:2])
    down = _dot(h.astype(BF16), wd_ref[...])
    n_lat = MLA_Q_LORA + MLA_KV_LORA
    cq = _rms(down[:, :MLA_Q_LORA], glq_ref[...]).astype(BF16)
    ckv = _rms(down[:, MLA_Q_LORA:n_lat], glkv_ref[...]).astype(BF16)
    kr2 = down[:, n_lat:n_lat + 128]
    kw2 = down[:, n_lat + 128:]
    tab = tab_ref[...]
    lo = lax.broadcasted_iota(jnp.int32, (1, 128), 1) < MLA_ROPE
    tab_r = pltpu.roll(tab, MLA_ROPE, 1)
    cos2 = jnp.where(lo, tab, tab_r)
    sin2 = jnp.where(lo, tab_r, tab)
    g = gains_ref[...]
    k_rot2 = kr2 * g[4:5] * cos2 + kw2 * g[5:6] * sin2
    kr_sq = jnp.where(lo, kr2 * kr2, 0.0)
    ones_col = (lax.broadcasted_iota(jnp.int32, (x_ref.shape[1], 128), 1) == 0).astype(BF16)
    for pair in range(MLA_HEADS // 2):
        qp = _dot(cq, wuq_ref[:, pair * 512:(pair + 1) * 512])
        tails = qp[:, 256:384]
        rot2 = tails * g[1:2] * cos2 + qp[:, 384:512] * g[2:3] * sin2
        tails_sq = tails * tails
        for j in range(2):
            hd = 2 * pair + j
            own = lo if j == 0 else jnp.logical_not(lo)
            qn = qp[:, j * 128:(j + 1) * 128]
            ssq = jnp.sum(qn * qn + jnp.where(own, tails_sq, 0.0), axis=-1, keepdims=True)
            rq = lax.rsqrt(ssq * (1.0 / MLA_QK) + EPS) * Q_SCALE
            q_ref[0, hd, :, :128] = (qn * g[0:1] * rq).astype(BF16)
            q_ref[0, hd, :, 128:] = (jnp.where(own, rot2, 0.0) * rq).astype(BF16)
            kvh = _dot(ckv, wukv_ref[:, hd * MLA_QK_PAD:(hd + 1) * MLA_QK_PAD])
            kn = kvh[:, :128]
            rk = lax.rsqrt(jnp.sum(kn * kn + kr_sq, axis=-1, keepdims=True) * (1.0 / MLA_QK) + EPS)
            k_ref[0, hd, :, :128] = (kn * g[3:4] * rk).astype(BF16)
            k_ref[0, hd, :, 128:] = (jnp.where(own, k_rot2, 0.0) * rk).astype(BF16)
            v_ref[0, hd, :, :MLA_V] = kvh[:, 128:].astype(BF16)
            v_ref[0, hd, :, MLA_V:] = ones_col


def _mla_proj(x, mod, gmix, wts, tab, tm):
    b, l, d = x.shape
    hh = MLA_HEADS
    per_batch_mod = mod.shape[0] > 1
    mod_map = (lambda bi, i: (bi, 0, 0)) if per_batch_mod else (lambda bi, i: (0, 0, 0))
    wd, glq, glkv, wuq, wukv, gains = wts
    return pl.pallas_call(
        _mla_proj_kernel,
        grid=(b, l // tm),
        in_specs=[
            pl.BlockSpec((1, tm, d), lambda bi, i: (bi, i, 0)),
            pl.BlockSpec((1, 6, d), mod_map),
            _full(gmix.shape), _full(wd.shape), _full(glq.shape), _full(glkv.shape),
            _full(wuq.shape), _full(wukv.shape), _full(gains.shape),
            pl.BlockSpec((tm, 128), lambda bi, i: (i, 0)),
        ],
        out_specs=[
            pl.BlockSpec((1, hh, tm, MLA_QK_PAD), lambda bi, i: (bi, 0, i, 0)),
            pl.BlockSpec((1, hh, tm, MLA_QK_PAD), lambda bi, i: (bi, 0, i, 0)),
            pl.BlockSpec((1, hh, tm, MLA_V_PAD), lambda bi, i: (bi, 0, i, 0)),
        ],
        out_shape=[
            jax.ShapeDtypeStruct((b, hh, l, MLA_QK_PAD), BF16),
            jax.ShapeDtypeStruct((b, hh, l, MLA_QK_PAD), BF16),
            jax.ShapeDtypeStruct((b, hh, l, MLA_V_PAD), BF16),
        ],
        compiler_params=_params(("parallel", "parallel")),
        name="mla_proj",
    )(x, mod, gmix, wd, glq, glkv, wuq, wukv, gains, tab)


def _flash_kernel(*refs, has_ctx, n_sub):
    if has_ctx:
        q_ref, k_ref, v_ref, kc_ref, vc_ref, o_ref, m_sc, acc_sc = refs
    else:
        q_ref, k_ref, v_ref, o_ref, m_sc, acc_sc = refs
    j = pl.program_id(3)
    tr = min(FLASH_ROWS, q_ref.shape[2])
    n_row = q_ref.shape[2] // tr

    def attend(kv_ref, vv_ref, n_chunk):
        ts = kv_ref.shape[2] // n_chunk
        pairs = [(c, r) for c in range(n_chunk) for r in range(n_row)]
        m_run = [m_sc[r * tr:(r + 1) * tr] for r in range(n_row)]
        s_of, p_of = {}, {}
        for t in range(len(pairs) + 2):
            if t < len(pairs):
                c, r = pairs[t]
                s_of[t] = _dot_nt(q_ref[0, 0, r * tr:(r + 1) * tr], kv_ref[0, 0, c * ts:(c + 1) * ts])
            if 0 <= t - 1 < len(pairs):
                c, r = pairs[t - 1]
                s = s_of.pop(t - 1)
                m_new = jnp.maximum(m_run[r], jnp.max(s, axis=-1, keepdims=True))
                p_of[t - 1] = (jnp.exp2(s - m_new).astype(BF16), jnp.exp2(m_run[r] - m_new))
                m_run[r] = m_new
            if 0 <= t - 2 < len(pairs):
                c, r = pairs[t - 2]
                p, alpha = p_of.pop(t - 2)
                rows = slice(r * tr, (r + 1) * tr)
                acc_sc[rows] = alpha * acc_sc[rows] + _dot(p, vv_ref[0, 0, c * ts:(c + 1) * ts])
        for r in range(n_row):
            m_sc[r * tr:(r + 1) * tr] = m_run[r]

    @pl.when(j == 0)
    def _():
        m_sc[...] = jnp.full(m_sc.shape, -1e30, F32)
        acc_sc[...] = jnp.zeros(acc_sc.shape, F32)
        if has_ctx:
            attend(kc_ref, vc_ref, 1)

    attend(k_ref, v_ref, n_sub)

    @pl.when(j == pl.num_programs(3) - 1)
    def _():
        acc = acc_sc[...]
        o_ref[0] = (acc[:, :MLA_V] / acc[:, MLA_V:MLA_V + 1]).astype(o_ref.dtype)


def _flash(q, k, v, kc, vc, tq, tk, n_sub):
    b, hh, sq, dq = q.shape
    lk = k.shape[2]
    dv = v.shape[3]
    has_ctx = kc is not None
    in_specs = [
        pl.BlockSpec((1, 1, tq, dq), lambda bi, h, i, j: (bi, h, i, 0)),
        pl.BlockSpec((1, 1, tk, dq), lambda bi, h, i, j: (bi, h, j, 0)),
        pl.BlockSpec((1, 1, tk, dv), lambda bi, h, i, j: (bi, h, j, 0)),
    ]
    args = [q, k, v]
    if has_ctx:
        lc = kc.shape[2]
        in_specs += [
            pl.BlockSpec((1, 1, lc, dq), lambda bi, h, i, j: (bi, h, 0, 0)),
            pl.BlockSpec((1, 1, lc, dv), lambda bi, h, i, j: (bi, h, 0, 0)),
        ]
        args += [kc, vc]
    return pl.pallas_call(
        functools.partial(_flash_kernel, has_ctx=has_ctx, n_sub=n_sub),
        grid=(b, hh, sq // tq, lk // tk),
        in_specs=in_specs,
        out_specs=pl.BlockSpec((1, tq, MLA_V), lambda bi, h, i, j: (bi, i, h)),
        out_shape=jax.ShapeDtypeStruct((b, sq, hh * MLA_V), BF16),
        scratch_shapes=[pltpu.VMEM((tq, 1), F32), pltpu.VMEM((tq, dv), F32)],
        compiler_params=_params(("parallel", "parallel", "parallel", "arbitrary")),
        name="mla_flash_ctx" if has_ctx else "mla_flash",
    )(*args)


def _post_ffn_kernel(x_ref, z_ref, mod_ref, wo_ref, gffn_ref, w1_ref, w3_ref, w2_ref, o_ref,
                     h_sc, acc_sc):
    f = pl.program_id(2)
    mod = mod_ref[0]

    @pl.when(f == 0)
    def _():
        x1 = x_ref[0] + mod[2:3] * _dot(z_ref[0], wo_ref[...])
        o_ref[0] = x1
        h_sc[...] = _norm_mod(x1, gffn_ref[...], mod[3:4], mod[4:5]).astype(BF16)
        acc_sc[...] = jnp.zeros(acc_sc.shape, F32)

    h = h_sc[...]
    a = _dot(h, w1_ref[...])
    g = (a * jax.nn.sigmoid(a)) * _dot(h, w3_ref[...])
    acc_sc[...] += _dot(g.astype(BF16), w2_ref[...])

    @pl.when(f == pl.num_programs(2) - 1)
    def _():
        o_ref[0] = o_ref[0] + mod[5:6] * acc_sc[...]


def _post_ffn(x, z, mod, wo, gffn, w1, w3, w2, tm, tf):
    b, l, d = x.shape
    dff = w1.shape[1]
    per_batch_mod = mod.shape[0] > 1
    mod_map = (lambda bi, i, f: (bi, 0, 0)) if per_batch_mod else (lambda bi, i, f: (0, 0, 0))
    return pl.pallas_call(
        _post_ffn_kernel,
        grid=(b, l // tm, dff // tf),
        in_specs=[
            pl.BlockSpec((1, tm, d), lambda bi, i, f: (bi, i, 0)),
            pl.BlockSpec((1, tm, d), lambda bi, i, f: (bi, i, 0)),
            pl.BlockSpec((1, 6, d), mod_map),
            pl.BlockSpec((d, d), lambda bi, i, f: (0, 0)),
            pl.BlockSpec((1, d), lambda bi, i, f: (0, 0)),
            pl.BlockSpec((d, tf), lambda bi, i, f: (0, f)),
            pl.BlockSpec((d, tf), lambda bi, i, f: (0, f)),
            pl.BlockSpec((tf, d), lambda bi, i, f: (f, 0)),
        ],
        out_specs=pl.BlockSpec((1, tm, d), lambda bi, i, f: (bi, i, 0)),
        out_shape=jax.ShapeDtypeStruct((b, l, d), F32),
        scratch_shapes=[pltpu.VMEM((tm, d), BF16), pltpu.VMEM((tm, d), F32)],
        compiler_params=_params(("parallel", "parallel", "arbitrary")),
        name="post_ffn",
    )(x, z, mod, wo, gffn, w1, w3, w2)


def _rwkv_feat_kernel(x_ref, xp_ref, xn_ref, mod_ref, gmix_ref, mix_ref, vec_ref, wr_ref, wk_ref, wv_ref,
                      dw1_ref, dw2_ref, ia1_ref, ia2_ref, g1_ref, g2_ref, hsum_ref,
                      r_ref, v_ref, g_ref, kk_ref, kd_ref, lw_ref, b_ref):
    i = pl.program_id(1)
    nt = pl.num_programs(1)
    mod = mod_ref[0]
    gmix = gmix_ref[...]
    tm, d = x_ref.shape[1], x_ref.shape[2]
    h = _norm_mod(x_ref[0], gmix, mod[0:1], mod[1:2])
    hp = _norm_mod(xp_ref[0], gmix, mod[0:1], mod[1:2])[7:8]
    hn = _norm_mod(xn_ref[0], gmix, mod[0:1], mod[1:2])[0:1]
    hp = jnp.where(i > 0, hp, 0.0)
    hn = jnp.where(i < nt - 1, hn, 0.0)
    row = lax.broadcasted_iota(jnp.int32, (tm, 1), 0)
    prev = jnp.where(row == 0, hp, pltpu.roll(h, 1, 0))
    nxt = jnp.where(row == tm - 1, hn, pltpu.roll(h, tm - 1, 0))
    xx = 0.5 * (prev + nxt) - h
    mix = mix_ref[...]
    vec = vec_ref[...]

    def mixed(jj):
        return (h + xx * mix[jj:jj + 1]).astype(BF16)

    r_ref[0] = _dot(mixed(0), wr_ref[...]).astype(r_ref.dtype)
    k = _dot(mixed(2), wk_ref[...])
    v_ref[0] = _dot(mixed(3), wv_ref[...]).astype(v_ref.dtype)
    g_ref[0] = _dot(jax.nn.sigmoid(_dot(mixed(5), g1_ref[...])).astype(BF16), g2_ref[...]).astype(g_ref.dtype)
    dd = _dot(jnp.tanh(_dot(mixed(1), dw1_ref[...])).astype(BF16), dw2_ref[...])
    aa = _dot(_dot(mixed(4), ia1_ref[...]).astype(BF16), ia2_ref[...])
    kkr = k * vec[0:1]
    kk = kkr * lax.rsqrt(_dot((kkr * kkr).astype(BF16), hsum_ref[...]) + 1e-12)
    kk_ref[0] = kk.astype(kk_ref.dtype)
    for dr in range(2):
        lw_ref[dr, 0] = -DECAY_SCALE * jax.nn.sigmoid(vec[2 + dr:3 + dr] + dd[:, dr * d:(dr + 1) * d])
        a = jax.nn.sigmoid(vec[4 + dr:5 + dr] + aa[:, dr * d:(dr + 1) * d])
        kd_ref[dr, 0] = (k * (1.0 + (a - 1.0) * vec[1:2])).astype(kd_ref.dtype)
        b_ref[dr, 0] = (kk * a).astype(b_ref.dtype)


def _rwkv_feat(x, mod, gmix, mix, vec, wts, tm):
    b, l, d = x.shape
    per_batch_mod = mod.shape[0] > 1
    mod_map = (lambda bi, i: (bi, 0, 0)) if per_batch_mod else (lambda bi, i: (0, 0, 0))
    t8 = tm // 8
    last8 = l // 8 - 1
    tok = pl.BlockSpec((1, tm, d), lambda bi, i: (bi, i, 0))
    tok2 = pl.BlockSpec((2, 1, tm, d), lambda bi, i: (0, bi, i, 0))
    one = jax.ShapeDtypeStruct((b, l, d), BF16)
    two = jax.ShapeDtypeStruct((2, b, l, d), BF16)
    two_f32 = jax.ShapeDtypeStruct((2, b, l, d), F32)
    return pl.pallas_call(
        _rwkv_feat_kernel,
        grid=(b, l // tm),
        in_specs=[
            tok,
            pl.BlockSpec((1, 8, d), lambda bi, i: (bi, jnp.maximum(i * t8 - 1, 0), 0)),
            pl.BlockSpec((1, 8, d), lambda bi, i: (bi, jnp.minimum((i + 1) * t8, last8), 0)),
            pl.BlockSpec((1, 6, d), mod_map),
            _full(gmix.shape), _full(mix.shape), _full(vec.shape),
        ] + [_full(w.shape) for w in wts],
        out_specs=[tok, tok, tok, tok, tok2, tok2, tok2],
        out_shape=[one, one, one, one, two, two_f32, two],
        compiler_params=_params(("parallel", "parallel")),
        name="rwkv_feat",
    )(x, x, x, mod, gmix, mix, vec, *wts)


def _wkv_kernel(r_ref, v_ref, kk_ref, kd_ref, lw_ref, b_ref, s0_ref, y_ref, sT_ref, st_sc):
    dr = pl.program_id(0)
    s = pl.program_id(1)
    nb = r_ref.shape[0]
    c = WKV_CHUNK
    n = RW_HEAD
    gw = WKV_GROUP * n
    d = r_ref.shape[2]
    groups = range(d // gw)

    @pl.when(s == 0)
    def _():
        st_sc[...] = s0_ref[0]

    sign = 1 - 2 * dr
    tt = lax.broadcasted_iota(jnp.int32, (c, c), 0)
    ss = lax.broadcasted_iota(jnp.int32, (c, c), 1)
    incl = (tt - ss) * sign >= 0
    t4 = lax.broadcasted_iota(jnp.int32, (c, gw), 0)
    s4 = lax.broadcasted_iota(jnp.int32, (c, gw), 1) % n
    ahead4 = (t4 - s4) * sign
    incl4 = ahead4 >= 0
    strict4 = ahead4 > 0
    eye4 = (t4 == s4).astype(F32)
    lane_head = lax.broadcasted_iota(jnp.int32, (c, 2 * n), 1) // n
    row_blk = lax.broadcasted_iota(jnp.int32, (gw, gw), 0) // n
    col_blk = lax.broadcasted_iota(jnp.int32, (gw, gw), 1) // n
    diag_blk = row_blk == col_blk

    def bdiag(x):
        xb = x.astype(BF16)
        zero = jnp.zeros((c, 2 * n), BF16)
        rows = []
        for hd in range(WKV_GROUP):
            tile = xb[:, (hd // 2) * 2 * n:(hd // 2 + 1) * 2 * n]
            tile = jnp.where(lane_head == hd % 2, tile, zero)
            rows.append(jnp.concatenate([tile, zero] if hd < 2 else [zero, tile], axis=1))
        return jnp.concatenate(rows, axis=0)

    tri = incl.astype(BF16)
    last = jnp.where(dr == 0, c - 1, 0)
    rowsel = lax.broadcasted_iota(jnp.int32, (c, 1), 0) == last
    at, rt, bt, kt, bw, kw, v, w_end = [], [], [], [], [], [], [], []
    for bi in range(nb):
        lw = lw_ref[0, bi]
        lw_hi = lw.astype(BF16)
        lw_lo = (lw - lw_hi.astype(F32)).astype(BF16)
        cum = _dot(tri, lw_hi) + _dot(tri, lw_lo)
        e_pos = jnp.exp(cum)
        e_neg = jnp.exp(-cum)
        we = jnp.sum(jnp.where(rowsel, e_pos, 0.0), axis=0, keepdims=True)
        rt.append(r_ref[bi].astype(F32) * e_pos)
        at.append(-kk_ref[bi].astype(F32) * jnp.exp(cum - lw))
        bt.append(b_ref[0, bi].astype(F32) * e_neg)
        kt.append(kd_ref[0, bi].astype(F32) * e_neg)
        bw.append(bt[bi] * we)
        kw.append(kt[bi] * we)
        v.append(v_ref[bi].astype(F32))
        w_end.append(we)

    units = [(bi, g) for bi in range(nb) for g in groups]
    nu = range(len(units))
    sl = [slice(g * gw, (g + 1) * gw) for _, g in units]
    bix = [bi for bi, _ in units]
    s0 = [st_sc[bi, g] for bi, g in units]
    ar = [jnp.concatenate([at[bix[i]][:, sl[i]], rt[bix[i]][:, sl[i]]], axis=0).astype(BF16) for i in nu]
    v_bd = [bdiag(v[bix[i]][:, sl[i]]) for i in nu]
    pb = [_dot_nt(ar[i], bdiag(bt[bix[i]][:, sl[i]])) for i in nu]
    pk = [_dot_nt(ar[i], bdiag(kt[bix[i]][:, sl[i]])) for i in nu]
    ars = [_dot_nt(ar[i], s0[i].astype(BF16)) for i in nu]
    l_ab = [jnp.where(strict4, pb[i][:c], 0.0) for i in nu]
    lakv = [_dot(jnp.where(strict4, pk[i][:c], 0.0).astype(BF16), v_bd[i]) for i in nu]
    m = [jnp.concatenate([jnp.where(incl4, pb[i][c:], 0.0), jnp.where(incl4, pk[i][c:], 0.0)],
                         axis=1).astype(BF16) for i in nu]
    tinv = [eye4 + l_ab[i] for i in nu]
    lp = [_dot(l_ab[i].astype(BF16), bdiag(l_ab[i])) for i in nu]
    for step in range(5):
        lp_b = [lp[i].astype(BF16) for i in nu]
        upd = [_dot(lp_b[i], bdiag(tinv[i])) for i in nu]
        if step < 4:
            lp = [_dot(lp_b[i], bdiag(lp[i])) for i in nu]
        tinv = [tinv[i] + upd[i] for i in nu]
    x = [ars[i][:c] + lakv[i] for i in nu]
    u = [_dot(tinv[i].astype(BF16), bdiag(x[i])) for i in nu]
    yg = [ars[i][c:] + _dot(m[i], jnp.concatenate([bdiag(u[i]), v_bd[i]], axis=0)) for i in nu]
    uv = [jnp.concatenate([u[i], v[bix[i]][:, sl[i]]], axis=0).astype(BF16) for i in nu]
    bkw = [jnp.concatenate([bw[bix[i]][:, sl[i]], kw[bix[i]][:, sl[i]]], axis=0).astype(BF16) for i in nu]
    s1 = [jnp.where(diag_blk, s0[i] * w_end[bix[i]][:, sl[i]] + _dot_tn(uv[i], bkw[i]), 0.0) for i in nu]
    for i, (bi, g) in enumerate(units):
        y_ref[0, bi, :, sl[i]] = yg[i].astype(y_ref.dtype)
        st_sc[bi, g] = s1[i]

    @pl.when(s == pl.num_programs(1) - 1)
    def _():
        sT_ref[0] = st_sc[...]


def _wkv(r, v, kk, kd, lw, bb, s0):
    b, l, d = r.shape
    c = WKV_CHUNK
    nc = l // c
    gw = WKV_GROUP * RW_HEAD
    ng = d // gw
    assert WKV_GROUP == 4 and RW_HEAD == 64 and d % gw == 0

    def cidx(dr, s):
        return s + dr * (nc - 1 - 2 * s)

    tok = pl.BlockSpec((b, c, d), lambda dr, s: (0, cidx(dr, s), 0))
    tok2 = pl.BlockSpec((1, b, c, d), lambda dr, s: (dr, 0, cidx(dr, s), 0))
    st = pl.BlockSpec((1, b, ng, gw, gw), lambda dr, s: (dr, 0, 0, 0, 0))
    return pl.pallas_call(
        _wkv_kernel,
        grid=(2, nc),
        in_specs=[tok, tok, tok, tok2, tok2, tok2, st],
        out_specs=[tok2, st],
        out_shape=[jax.ShapeDtypeStruct((2, b, l, d), BF16),
                   jax.ShapeDtypeStruct((2, b, ng, gw, gw), F32)],
        scratch_shapes=[pltpu.VMEM((b, ng, gw, gw), F32)],
        compiler_params=_params(("parallel", "arbitrary")),
        name="wkv7",
    )(r, v, kk, kd, lw, bb, s0)


def _rwkv_out_kernel(y_ref, r_ref, kd_ref, v_ref, g_ref, vec_ref, hred_ref, hexp_ref, z_ref):
    vec = vec_ref[...]
    inv_n = 1.0 / RW_HEAD

    def head_sum(t):
        return _dot(_dot(t.astype(BF16), hred_ref[...]).astype(BF16), hexp_ref[...])

    y = y_ref[0, 0].astype(F32) + y_ref[1, 0].astype(F32)
    mu = head_sum(y) * inv_n
    yc = y - mu
    var = head_sum(yc * yc) * inv_n
    yn = yc * lax.rsqrt(var + RW_GN_EPS) * vec[1:2] + vec[2:3]
    rk = r_ref[0].astype(F32) * (kd_ref[0, 0].astype(F32) + kd_ref[1, 0].astype(F32)) * vec[0:1]
    bonus = head_sum(rk) * v_ref[0].astype(F32)
    z_ref[0] = ((yn + bonus) * g_ref[0].astype(F32)).astype(z_ref.dtype)


def _rwkv_out(y, r, kd, v, g, vec, hred, hexp, tm):
    b, l, d = r.shape
    tok = pl.BlockSpec((1, tm, d), lambda bi, i: (bi, i, 0))
    tok2 = pl.BlockSpec((2, 1, tm, d), lambda bi, i: (0, bi, i, 0))
    return pl.pallas_call(
        _rwkv_out_kernel,
        grid=(b, l // tm),
        in_specs=[tok2, tok, tok2, tok, tok, _full(vec.shape), _full(hred.shape), _full(hexp.shape)],
        out_specs=tok,
        out_shape=jax.ShapeDtypeStruct((b, l, d), BF16),
        compiler_params=_params(("parallel", "parallel")),
        name="rwkv_out",
    )(y, r, kd, v, g, vec, hred, hexp)


def _rope_table(n_tokens):
    rows = n_tokens // GRID_W
    row = jnp.broadcast_to(jnp.arange(rows, dtype=F32)[:, None], (rows, GRID_W)).reshape(-1)
    col = jnp.broadcast_to(jnp.arange(GRID_W, dtype=F32)[None, :], (rows, GRID_W)).reshape(-1)
    inv = ROPE_BASE ** (-jnp.arange(ROPE_FREQ, dtype=F32) / ROPE_FREQ)
    ar, ac = row[:, None] * inv, col[:, None] * inv
    cr, sr, cc, sc = jnp.cos(ar), jnp.sin(ar), jnp.cos(ac), jnp.sin(ac)
    return jnp.concatenate([cr, cr, cc, cc, -sr, sr, -sc, sc], axis=1)


def _pad8(x):
    return jnp.pad(x, ((0, 8 - x.shape[0]), (0, 0)))


def kernel(x, c, ctx, c_ctx, ada_w, ada_b, norm_mix, norm_ffn, ffn_w1, ffn_w3, ffn_w2, mla_w_dqkv, mla_g_q_lora, mla_g_kv_lora, mla_w_uq, mla_w_ukv, mla_g_qn, mla_g_kn, mla_w_o, rw_mix, rw_w_r, rw_w_k, rw_w_v, rw_w_o, rw_k_k, rw_k_a, rw_r_k, rw_decay_w0, rw_decay_w1, rw_decay_w2, rw_iclr_a0, rw_iclr_a1, rw_iclr_a2, rw_gate_g1, rw_gate_g2, rw_gn_w, rw_gn_b):
    bsz, seq, d = x.shape
    n_ctx = ctx.shape[1]
    hh = MLA_HEADS

    cc = _pad8(jnp.concatenate([c, c_ctx[None, :]], axis=0))
    mod = _adaln_mod(cc, ada_w, ada_b)
    mod_l = [mod[i, :bsz].reshape(bsz, 6, d) for i in range(2)]
    mod_c = [mod[i, bsz:bsz + 1].reshape(1, 6, d) for i in range(2)]

    swap = np.arange(MLA_ROPE) ^ ROPE_FREQ
    wdq = mla_w_dqkv[0]
    rope0 = MLA_Q_LORA + MLA_KV_LORA
    kr_w, ks_w = wdq[:, rope0:], wdq[:, rope0 + swap]
    wd = jnp.concatenate([wdq[:, :rope0], kr_w, kr_w, ks_w, ks_w], axis=1).astype(BF16)
    wuq = mla_w_uq[0].reshape(MLA_Q_LORA, hh, MLA_QK)
    nope, tail = wuq[:, :, :MLA_NOPE], wuq[:, :, MLA_NOPE:]
    tail_sw = tail[:, :, swap]
    wuq = jnp.concatenate([jnp.concatenate([nope[:, a], nope[:, a + 1], tail[:, a], tail[:, a + 1],
                                            tail_sw[:, a], tail_sw[:, a + 1]], axis=1)
                           for a in range(0, hh, 2)], axis=1).astype(BF16)
    wukv = mla_w_ukv[0].astype(BF16)

    def gain_rows(gn):
        tail_g = gn[MLA_NOPE:]
        return [gn[:MLA_NOPE], jnp.concatenate([tail_g, tail_g]), jnp.concatenate([tail_g[swap], tail_g[swap]])]

    gains = _pad8(jnp.stack(gain_rows(mla_g_qn[0]) + gain_rows(mla_g_kn[0])))
    proj_w = (wd, mla_g_q_lora[0][None, :], mla_g_kv_lora[0][None, :], wuq, wukv, gains)
    tab_l = _rope_table(seq)
    tab_c = jnp.concatenate([jnp.ones((n_ctx, MLA_ROPE), F32), jnp.zeros((n_ctx, MLA_ROPE), F32)], axis=1)
    gmix0 = norm_mix[0][None, :]
    tm_l = min(512, seq)
    tm_c = min(256, n_ctx)
    q_l, k_l, v_l = _mla_proj(x, mod_l[0], gmix0, proj_w, tab_l, tm_l)
    q_c, k_c, v_c = _mla_proj(ctx, mod_c[0], gmix0, proj_w, tab_c, tm_c)
    tk_l = min(8192, seq)
    o_l = _flash(q_l, k_l, v_l, k_c, v_c, min(2048, seq), tk_l, max(tk_l // FLASH_KEYS, 1))
    o_c = _flash(q_c, k_c, v_c, None, None, tm_c, n_ctx, 1)

    w1 = ffn_w1.astype(BF16)
    w3 = ffn_w3.astype(BF16)
    w2 = ffn_w2.astype(BF16)
    tf = 256
    wo0 = mla_w_o[0].astype(BF16)
    dff = w1.shape[2]
    tf_l = dff // 2 if dff % 256 == 0 else tf
    tm_ffn = min(1024, seq)
    x1 = _post_ffn(x, o_l, mod_l[0], wo0, norm_ffn[0][None, :], w1[0], w3[0], w2[0], tm_ffn, tf_l)
    ctx1 = _post_ffn(ctx, o_c, mod_c[0], wo0, norm_ffn[0][None, :], w1[0], w3[0], w2[0], tm_c, tf_l)

    zpad = RW_GATE_PAD - RW_GATE_LORA
    dw2 = rw_decay_w2[0]
    ia2 = rw_iclr_a2[0]
    zero = jnp.zeros_like(dw2[0])
    feat_w = (
        rw_w_r[0].astype(BF16), rw_w_k[0].astype(BF16), rw_w_v[0].astype(BF16),
        jnp.concatenate([rw_decay_w1[0, 0], rw_decay_w1[0, 1]], axis=1).astype(BF16),
        jnp.concatenate([jnp.concatenate([dw2[0], zero], 1), jnp.concatenate([zero, dw2[1]], 1)], 0).astype(BF16),
        jnp.concatenate([rw_iclr_a1[0, 0], rw_iclr_a1[0, 1]], axis=1).astype(BF16),
        jnp.concatenate([jnp.concatenate([ia2[0], zero], 1), jnp.concatenate([zero, ia2[1]], 1)], 0).astype(BF16),
        jnp.pad(rw_gate_g1[0], ((0, 0), (0, zpad))).astype(BF16),
        jnp.pad(rw_gate_g2[0], ((0, zpad), (0, 0))).astype(BF16),
    )
    head_of = np.arange(d) // RW_HEAD
    hsum = jnp.asarray(head_of[:, None] == head_of[None, :], BF16)
    vec = _pad8(jnp.stack([rw_k_k[0], rw_k_a[0], rw_decay_w0[0, 0], rw_decay_w0[0, 1],
                           rw_iclr_a0[0, 0], rw_iclr_a0[0, 1]]))
    gmix1 = norm_mix[1][None, :]
    tmf = 512
    r_l, v_l2, g_l, kk_l, kd_l, lw_l, b_l = _rwkv_feat(x1, mod_l[1], gmix1, rw_mix[0], vec, feat_w + (hsum,), min(tmf, seq))
    r_c, v_c2, _, kk_c, kd_c, lw_c, b_c = _rwkv_feat(ctx1, mod_c[1], gmix1, rw_mix[0], vec, feat_w + (hsum,), min(tmf, n_ctx))
    gw = WKV_GROUP * RW_HEAD
    s_zero = jnp.zeros((2, bsz, d // gw, gw, gw), F32)
    _, s_ctx = _wkv(r_c, v_c2, kk_c, kd_c, lw_c, b_c, s_zero)
    y, _ = _wkv(r_l, v_l2, kk_l, kd_l, lw_l, b_l, s_ctx)
    ovec = _pad8(jnp.stack([rw_r_k[0].reshape(-1), rw_gn_w[0], rw_gn_b[0]]))
    hred = jnp.asarray(head_of[:, None] == np.arange(128)[None, :], BF16)
    z = _rwkv_out(y, r_l, kd_l, v_l2, g_l, ovec, hred, hred.T, min(256, seq))
    return _post_ffn(x1, z, mod_l[1], rw_w_o[0].astype(BF16), norm_ffn[1][None, :], w1[1], w3[1], w2[1], tm_ffn, tf_l)
```
